```python
import jax, jax.numpy as jnp
from jax import lax
import numpy as np

D_MODEL = 2048
BATCH = 4
SEQ = 8192
DEPTH = 1

PLE_DIM = 256
CONV_GROUPS = 8
CONV_GROUP_DIM = 128
CONV_CH = CONV_GROUPS * CONV_GROUP_DIM
CONV_WIDTH = 3
N_HEADS = 8
QK_NOPE = 128
QK_ROPE = 64
V_HEAD = 128
Q_LORA = 512
KV_LORA = 256
ATTN_CH = N_HEADS * V_HEAD
MIX_WIDTH = CONV_CH + ATTN_CH
ROPE_THETA = 10000.0
Q_BLOCK = 128
PEER_HEADS = 8
PEER_TOPK = 16
N_KEYS = 128
N_EXPERTS = N_KEYS * N_KEYS
D_QUERY = 256
D_HALF = D_QUERY // 2
PEER_BLOCK = 32
EPS = 1e-6
IN_COLS = 3 * CONV_CH + Q_LORA + KV_LORA + QK_ROPE
IN_SPLITS = [CONV_CH, 2 * CONV_CH, 3 * CONV_CH, 3 * CONV_CH + Q_LORA, 3 * CONV_CH + Q_LORA + KV_LORA]

kernel_name = 'hybrid_shortconv_mla_peer_block'


def rmsnorm(x, g):
    xf = x.astype(jnp.float32)
    y = xf * lax.rsqrt(jnp.mean(xf * xf, axis=-1, keepdims=True) + EPS)
    return (y * g.astype(jnp.float32)).astype(x.dtype)


def head_rmsnorm(y, g, n_groups):
    B, S, C = y.shape
    yf = y.reshape(B, S, n_groups, C // n_groups).astype(jnp.float32)
    yf = yf * lax.rsqrt(jnp.mean(yf * yf, axis=-1, keepdims=True) + EPS)
    return (yf.reshape(B, S, C) * g.astype(jnp.float32)).astype(y.dtype)


def rope_tables(positions):
    inv_freq = ROPE_THETA ** (-(jnp.arange(0, QK_ROPE, 2, dtype=jnp.float32) / QK_ROPE))
    ang = positions.astype(jnp.float32)[..., None] * inv_freq
    return jnp.cos(ang), jnp.sin(ang)


def apply_rope(x, cos, sin):
    x1, x2 = jnp.split(x.astype(jnp.float32), 2, axis=-1)
    return jnp.concatenate([x1 * cos - x2 * sin, x2 * cos + x1 * sin], axis=-1).astype(x.dtype)


def short_conv(xin, b_gate, c_gate, w_conv):
    u = c_gate * xin
    S = u.shape[1]
    u1 = jnp.pad(u, ((0, 0), (1, 0), (0, 0)))[:, :S]
    u2 = jnp.pad(u, ((0, 0), (2, 0), (0, 0)))[:, :S]
    y = w_conv[2] * u + w_conv[1] * u1 + w_conv[0] * u2
    return b_gate * y


def causal_attention(q, k, v):
    B, S, H, Dq = q.shape
    nb = S // Q_BLOCK
    scale = Dq ** -0.5
    qb = q.reshape(B, nb, Q_BLOCK, H, Dq).swapaxes(0, 1)
    kpos = jnp.arange(S)

    def block(args):
        i, qi = args
        s = jnp.einsum('bqhd,bkhd->bhqk', qi, k, preferred_element_type=jnp.float32) * scale
        qpos = i * Q_BLOCK + jnp.arange(Q_BLOCK)
        mask = kpos[None, :] <= qpos[:, None]
        s = jnp.where(mask, s, jnp.finfo(jnp.float32).min)
        pr = jax.nn.softmax(s, axis=-1)
        return jnp.einsum('bhqk,bkhd->bqhd', pr.astype(v.dtype), v)

    out = lax.map(block, (jnp.arange(nb), qb))
    return out.swapaxes(0, 1).reshape(B, S, H * V_HEAD)


def peer(x, w_pq, sub_keys, u_tab, v_tab):
    B, S, D = x.shape
    q = (x @ w_pq).reshape(B, S, PEER_HEADS, 2, D_HALF)
    s = jnp.einsum('bshpd,hpnd->bshpn', q, sub_keys, preferred_element_type=jnp.float32)
    top_v, top_i = lax.top_k(s, PEER_TOPK)
    cand = top_v[..., 0, :, None] + top_v[..., 1, None, :]
    cand = cand.reshape(B, S, PEER_HEADS, PEER_TOPK * PEER_TOPK)
    best, pos = lax.top_k(cand, PEER_TOPK)
    i1 = jnp.take_along_axis(top_i[..., 0, :], pos // PEER_TOPK, axis=-1)
    i2 = jnp.take_along_axis(top_i[..., 1, :], pos % PEER_TOPK, axis=-1)
    expert = i1 * N_KEYS + i2
    gate = jax.nn.softmax(best, axis=-1)

    nb = S // PEER_BLOCK

    def to_blocks(a):
        return a.reshape(B, nb, PEER_BLOCK, *a.shape[2:]).swapaxes(0, 1)

    def block(args):
        xb, eb, gb = args
        u = u_tab[eb]
        hdn = jnp.einsum('bthkd,btd->bthk', u, xb, preferred_element_type=jnp.float32)
        w = (gb * jax.nn.gelu(hdn, approximate=False)).astype(xb.dtype)
        return jnp.einsum('bthk,bthkd->btd', w, v_tab[eb])

    out = lax.map(block, (to_blocks(x), to_blocks(expert), to_blocks(gate)))
    return out.swapaxes(0, 1).reshape(B, S, D)


def setup_inputs(seed: int = 0) -> dict:
    key = jax.random.key(seed)
    ks = jax.random.split(key, 24)
    f32 = jnp.float32

    def nrm(k, shape, scale):
        return jax.random.normal(k, shape, f32) * scale

    def gain(k, shape):
        return 1.0 + 0.02 * jax.random.normal(k, shape, f32)

    x = jax.random.normal(ks[0], (BATCH, SEQ, D_MODEL), f32)
    p = jax.random.normal(ks[1], (DEPTH, BATCH, SEQ, PLE_DIM), f32)
    positions = jnp.broadcast_to(jnp.arange(SEQ, dtype=jnp.int32)[None, :], (BATCH, SEQ))
    return {
        'x': x,
        'p': p,
        'positions': positions,
        'attn_norm': gain(ks[2], (DEPTH, D_MODEL)),
        'w_in': nrm(ks[3], (DEPTH, D_MODEL, IN_COLS), D_MODEL ** -0.5),
        'conv_w': nrm(ks[4], (DEPTH, CONV_WIDTH, CONV_CH), CONV_WIDTH ** -0.5),
        'q_norm': gain(ks[5], (DEPTH, Q_LORA)),
        'w_uq': nrm(ks[6], (DEPTH, Q_LORA, N_HEADS * (QK_NOPE + QK_ROPE)), Q_LORA ** -0.5),
        'kv_norm': gain(ks[7], (DEPTH, KV_LORA)),
        'w_ukv': nrm(ks[8], (DEPTH, KV_LORA, N_HEADS * (QK_NOPE + V_HEAD)), KV_LORA ** -0.5),
        'conv_out_norm': gain(ks[9], (DEPTH, CONV_CH)),
        'attn_out_norm': gain(ks[10], (DEPTH, ATTN_CH)),
        'w_out': nrm(ks[11], (DEPTH, MIX_WIDTH, D_MODEL), MIX_WIDTH ** -0.5),
        'ffn_norm': gain(ks[12], (DEPTH, D_MODEL)),
        'w_pq': nrm(ks[13], (DEPTH, D_MODEL, PEER_HEADS * D_QUERY), D_MODEL ** -0.5),
        'sub_keys': nrm(ks[14], (DEPTH, PEER_HEADS, 2, N_KEYS, D_HALF), D_HALF ** -0.5),
        'u_tab': nrm(ks[15], (DEPTH, N_EXPERTS, D_MODEL), D_MODEL ** -0.5),
        'v_tab': nrm(ks[16], (DEPTH, N_EXPERTS, D_MODEL), PEER_TOPK ** -0.5),
        'ple_norm': gain(ks[17], (DEPTH, D_MODEL)),
        'w_ple_gate': nrm(ks[18], (DEPTH, D_MODEL, D_MODEL), D_MODEL ** -0.5),
        'w_ple_proj': nrm(ks[19], (DEPTH, PLE_DIM, D_MODEL), PLE_DIM ** -0.5),
        'final_norm': gain(ks[20], (D_MODEL,)),
    }


def reference(x, p, positions, attn_norm, w_in, conv_w, q_norm, w_uq, kv_norm, w_ukv,
              conv_out_norm, attn_out_norm, w_out, ffn_norm, w_pq, sub_keys, u_tab, v_tab,
              ple_norm, w_ple_gate, w_ple_proj, final_norm):
    B, S, _ = x.shape
    cos, sin = rope_tables(positions)
    h = x
    for i in range(DEPTH):
        a = rmsnorm(h, attn_norm[i])
        proj = a @ w_in[i]
        xin, b_g, c_g, c_q, c_kv, k_r = jnp.split(proj, IN_SPLITS, axis=-1)

        conv_o = short_conv(xin, b_g, c_g, conv_w[i])

        q = (rmsnorm(c_q, q_norm[i]) @ w_uq[i]).reshape(B, S, N_HEADS, QK_NOPE + QK_ROPE)
        q_nope, q_rope = q[..., :QK_NOPE], q[..., QK_NOPE:]
        kv = (rmsnorm(c_kv, kv_norm[i]) @ w_ukv[i]).reshape(B, S, N_HEADS, QK_NOPE + V_HEAD)
        k_nope, v = kv[..., :QK_NOPE], kv[..., QK_NOPE:]
        q_rope = apply_rope(q_rope, cos[:, :, None, :], sin[:, :, None, :])
        k_r = apply_rope(k_r, cos, sin)
        qf = jnp.concatenate([q_nope, q_rope], axis=-1)
        kf = jnp.concatenate([k_nope, jnp.broadcast_to(k_r[:, :, None, :], (B, S, N_HEADS, QK_ROPE))], axis=-1)
        attn_o = causal_attention(qf, kf, v)

        mixed = jnp.concatenate([head_rmsnorm(conv_o, conv_out_norm[i], CONV_GROUPS),
                                 head_rmsnorm(attn_o, attn_out_norm[i], N_HEADS)], axis=-1)
        h = h + mixed @ w_out[i]

        f = rmsnorm(h, ffn_norm[i])
        h = h + peer(f, w_pq[i], sub_keys[i], u_tab[i], v_tab[i])

        gate = jax.nn.sigmoid((rmsnorm(h, ple_norm[i]) @ w_ple_gate[i]).astype(jnp.float32))
        h = h + ((p[i] @ w_ple_proj[i]).astype(jnp.float32) * gate).astype(h.dtype)
    return rmsnorm(h, final_norm)
```

```python
import functools

import jax
import jax.numpy as jnp
from jax import lax
from jax.experimental import pallas as pl
from jax.experimental.pallas import tpu as pltpu

EPS = 1e-6
ROPE_THETA = 10000.0
CONV_GROUP_DIM = 128
N_HEADS = 8
QK_NOPE = 128
QK_ROPE = 64
V_HEAD = 128
HEAD_PAD = 256
PEER_HEADS = 8
PEER_TOPK = 16
N_KEYS = 128
LANES = 128
NEG_BIG = -1e30

VMEM_LIMIT = 56 * 1024 * 1024

F32 = jnp.float32
BF16 = jnp.bfloat16


def _cparams(semantics):
    return pltpu.CompilerParams(dimension_semantics=semantics, vmem_limit_bytes=VMEM_LIMIT)


def _resident(shape):
    nd = len(shape)
    return pl.BlockSpec(shape, lambda *_: (0,) * nd, pipeline_mode=pl.Buffered(1))


def _rms(xf, g):
    return xf * lax.rsqrt(jnp.mean(xf * xf, axis=-1, keepdims=True) + EPS) * g


def _in_proj_kernel(x_ref, pos_ref, an_ref, win_ref, cw_ref, qn_ref, wuq_ref, kvn_ref, wukv_ref,
                    con_ref, freq_ref, sign_ref,
                    conv_ref, q_ref, k_ref, v_ref, carry_ref, *, tiles_per_seq, conv_ch, q_lora,
                    kv_lora):
    i = pl.program_id(0)
    tm = x_ref.shape[0]

    @pl.when(i % tiles_per_seq == 0)
    def _():
        carry_ref[...] = jnp.zeros_like(carry_ref)

    a = _rms(x_ref[...], an_ref[...]).astype(BF16)

    xin = jnp.dot(a, win_ref[:, 0:conv_ch], preferred_element_type=F32)
    c_g = jnp.dot(a, win_ref[:, 2 * conv_ch:3 * conv_ch], preferred_element_type=F32)
    u = c_g * xin
    prev = carry_ref[...]
    row = lax.broadcasted_iota(jnp.int32, u.shape, 0)
    u1 = jnp.where(row == 0, prev[7:8], pltpu.roll(u, 1, 0))
    u2 = jnp.where(row == 0, prev[6:7], jnp.where(row == 1, prev[7:8], pltpu.roll(u, 2, 0)))
    carry_ref[...] = u[tm - 8:tm]
    y = cw_ref[2:3] * u + cw_ref[1:2] * u1 + cw_ref[0:1] * u2
    b_g = jnp.dot(a, win_ref[:, conv_ch:2 * conv_ch], preferred_element_type=F32)
    conv_o = b_g * y
    for g in range(conv_ch // CONV_GROUP_DIM):
        sl = slice(g * CONV_GROUP_DIM, (g + 1) * CONV_GROUP_DIM)
        conv_ref[:, sl] = _rms(conv_o[:, sl], con_ref[:, sl]).astype(BF16)

    m0 = 3 * conv_ch
    lat = jnp.dot(a, win_ref[:, m0:], preferred_element_type=F32)
    ang = pos_ref[...].astype(F32) * freq_ref[...]
    cos_t = jnp.cos(ang) * jnp.abs(sign_ref[...])
    sin_t = jnp.sin(ang) * sign_ref[...]

    def rope(blk):
        return blk * cos_t + pltpu.roll(blk, QK_ROPE, 1) * sin_t

    cq = _rms(lat[:, 0:q_lora], qn_ref[...]).astype(BF16)
    qf = jnp.dot(cq, wuq_ref[...], preferred_element_type=F32)
    ckv = _rms(lat[:, q_lora:q_lora + kv_lora], kvn_ref[...]).astype(BF16)
    kvf = jnp.dot(ckv, wukv_ref[...], preferred_element_type=F32)
    kr = rope(lat[:, q_lora + kv_lora:]).astype(BF16)
    for h in range(N_HEADS):
        o = h * HEAD_PAD
        q_ref[:, o:o + QK_NOPE] = qf[:, o:o + QK_NOPE].astype(BF16)
        q_ref[:, o + QK_NOPE:o + HEAD_PAD] = rope(qf[:, o + QK_NOPE:o + HEAD_PAD]).astype(BF16)
        k_ref[:, o:o + QK_NOPE] = kvf[:, h * QK_NOPE:(h + 1) * QK_NOPE].astype(BF16)
        k_ref[:, o + QK_NOPE:o + HEAD_PAD] = kr
    v_ref[...] = kvf[:, N_HEADS * QK_NOPE:].astype(BF16)


def _in_proj(x2, pos2, attn_norm, w_in_ext, conv_w, q_norm, w_uq_ext, kv_norm, w_ukv_p,
             conv_out_norm, freq_row, sign_row, *, seq, tm):
    n, d = x2.shape
    conv_ch = conv_w.shape[-1]
    q_lora, kv_lora = q_norm.shape[-1], kv_norm.shape[-1]
    assert seq % tm == 0 and n % seq == 0
    tok = lambda w: pl.BlockSpec((tm, w), lambda i: (i, 0))
    kern = functools.partial(_in_proj_kernel, tiles_per_seq=seq // tm, conv_ch=conv_ch,
                             q_lora=q_lora, kv_lora=kv_lora)
    return pl.pallas_call(
        kern,
        grid=(n // tm,),
        in_specs=[tok(d), tok(1), _resident((1, d)), _resident(w_in_ext.shape),
                  _resident(conv_w.shape), _resident((1, q_lora)), _resident(w_uq_ext.shape),
                  _resident((1, kv_lora)), _resident(w_ukv_p.shape), _resident((1, conv_ch)),
                  _resident((1, LANES)), _resident((1, LANES))],
        out_specs=[tok(conv_ch), tok(N_HEADS * HEAD_PAD), tok(N_HEADS * HEAD_PAD),
                   tok(N_HEADS * V_HEAD)],
        out_shape=[jax.ShapeDtypeStruct((n, conv_ch), BF16),
                   jax.ShapeDtypeStruct((n, N_HEADS * HEAD_PAD), BF16),
                   jax.ShapeDtypeStruct((n, N_HEADS * HEAD_PAD), BF16),
                   jax.ShapeDtypeStruct((n, N_HEADS * V_HEAD), BF16)],
        scratch_shapes=[pltpu.VMEM((8, conv_ch), F32)],
        compiler_params=_cparams(("arbitrary",)),
        name="in_proj",
    )(x2, pos2, attn_norm, w_in_ext, conv_w, q_norm, w_uq_ext, kv_norm, w_ukv_p, conv_out_norm,
      freq_row, sign_row)


def _attn_kernel(q_ref, k_ref, v_ref, g_ref, o_ref, m_ref, l_ref, acc_ref, *, tk, scale):
    qi = pl.program_id(2)
    tq = q_ref.shape[0]
    n_diag = tq // tk
    q = q_ref[...]
    m_ref[...] = jnp.full_like(m_ref, NEG_BIG)
    l_ref[...] = jnp.zeros_like(l_ref)
    acc_ref[...] = jnp.zeros_like(acc_ref)

    def step(kb, mask_offset):
        start = pl.multiple_of(kb * tk, tk)
        k = k_ref[pl.ds(start, tk), :]
        v = v_ref[pl.ds(start, tk), :]
        s = lax.dot_general(q, k, (((1,), (1,)), ((), ())), preferred_element_type=F32) * scale
        if mask_offset is not None:
            r = lax.broadcasted_iota(jnp.int32, s.shape, 0)
            c = lax.broadcasted_iota(jnp.int32, s.shape, 1)
            s = jnp.where(c + mask_offset <= r, s, NEG_BIG)
        m_old = m_ref[...]
        m_new = jnp.maximum(m_old, jnp.max(s, axis=-1, keepdims=True))
        alpha = jnp.exp(m_old - m_new)
        p = jnp.exp(s - m_new)
        l_ref[...] = alpha * l_ref[...] + jnp.sum(p, axis=-1, keepdims=True)
        acc_ref[...] = alpha * acc_ref[...] + jnp.dot(p.astype(BF16), v, preferred_element_type=F32)
        m_ref[...] = m_new

    n_full = qi * n_diag

    def body(kb, carry):
        step(kb, None)
        return carry

    lax.fori_loop(0, n_full, body, 0)
    for j in range(n_diag):
        step(n_full + j, j * tk)

    o = acc_ref[...] / l_ref[...]
    o_ref[...] = _rms(o, g_ref[...]).astype(BF16)


def _attention(q_pad, k_pad, v, attn_out_norm, *, batch, seq, tq, tk):
    n = q_pad.shape[0]
    assert seq % tq == 0 and tq % tk == 0
    nq = seq // tq
    scale = float(QK_NOPE + QK_ROPE) ** -0.5
    kern = functools.partial(_attn_kernel, tk=tk, scale=scale)
    return pl.pallas_call(
        kern,
        grid=(batch, N_HEADS, nq),
        in_specs=[pl.BlockSpec((tq, HEAD_PAD), lambda b, h, i: (b * nq + i, h)),
                  pl.BlockSpec((seq, HEAD_PAD), lambda b, h, i: (b, h)),
                  pl.BlockSpec((seq, V_HEAD), lambda b, h, i: (b, h)),
                  pl.BlockSpec((1, V_HEAD), lambda b, h, i: (0, h))],
        out_specs=pl.BlockSpec((tq, V_HEAD), lambda b, h, i: (b * nq + i, h)),
        out_shape=jax.ShapeDtypeStruct((n, N_HEADS * V_HEAD), BF16),
        scratch_shapes=[pltpu.VMEM((tq, 1), F32), pltpu.VMEM((tq, 1), F32),
                        pltpu.VMEM((tq, V_HEAD), F32)],
        compiler_params=_cparams(("arbitrary", "arbitrary", "arbitrary")),
        name="attention",
    )(q_pad, k_pad, v, attn_out_norm)


def _out_proj_kernel(conv_ref, attn_ref, x_ref, wo_ref, fn_ref, wpqt_ref, keys_ref,
                     h1_ref, ft_ref, sc_ref):
    conv_ch = conv_ref.shape[1]
    mix = jnp.dot(conv_ref[...], wo_ref[0:conv_ch, :], preferred_element_type=F32)
    mix += jnp.dot(attn_ref[...], wo_ref[conv_ch:, :], preferred_element_type=F32)
    h1 = x_ref[...] + mix
    h1_ref[...] = h1
    ft = _rms(h1, fn_ref[...]).T.astype(BF16)
    ft_ref[...] = ft
    qpt = jnp.dot(wpqt_ref[...], ft, preferred_element_type=F32).astype(BF16)
    for hp in range(keys_ref.shape[0]):
        sc_ref[hp] = jnp.dot(keys_ref[hp], qpt[hp * N_KEYS:(hp + 1) * N_KEYS, :],
                             preferred_element_type=F32)


def _out_proj(conv_n, attn_n, x2, w_out, ffn_norm, w_pq_t, keys, *, tm):
    n, d = x2.shape
    nhp = keys.shape[0]
    tok = lambda w: pl.BlockSpec((tm, w), lambda i: (i, 0))
    return pl.pallas_call(
        _out_proj_kernel,
        grid=(n // tm,),
        in_specs=[tok(conv_n.shape[1]), tok(attn_n.shape[1]), tok(d), _resident(w_out.shape),
                  _resident((1, d)), _resident(w_pq_t.shape), _resident(keys.shape)],
        out_specs=[tok(d), pl.BlockSpec((d, tm), lambda i: (0, i)),
                   pl.BlockSpec((nhp, N_KEYS, tm), lambda i: (0, 0, i))],
        out_shape=[jax.ShapeDtypeStruct((n, d), F32), jax.ShapeDtypeStruct((d, n), BF16),
                   jax.ShapeDtypeStruct((nhp, N_KEYS, n), F32)],
        compiler_params=_cparams(("arbitrary",)),
        name="out_proj",
    )(conv_n, attn_n, x2, w_out, ffn_norm, w_pq_t, keys)


def _top16(s):
    c = s.shape[1]
    rows = lax.broadcasted_iota(jnp.int32, (PEER_TOPK, c), 0)
    rank = jnp.full(s.shape, float(PEER_TOPK), F32)
    vals = jnp.zeros((PEER_TOPK, c), F32)
    x = s
    for k in range(PEER_TOPK):
        m = jnp.max(x, axis=0, keepdims=True)
        hit = x == m
        rank = jnp.where(hit, float(k), rank)
        x = jnp.where(hit, -jnp.inf, x)
        vals = jnp.where(rows == k, m, vals)
    return rank, vals


def _peer_gate_kernel(sc_ref, e1_ref, l1_ref, e2_ref, r2_ref):
    tm = sc_ref.shape[2]

    def one(h, c0):
        lanes = pl.ds(pl.multiple_of(c0, LANES), LANES)
        s1 = sc_ref[2 * h, :, lanes]
        s2 = sc_ref[2 * h + 1, :, lanes]
        rank1, v1 = _top16(s1)
        rank2, v2 = _top16(s2)
        cand = jnp.concatenate([v1 + v2[0:1]] + [v1[0:8] + v2[b:b + 1] for b in range(1, 8)]
                               + [v1[0:1] + v2[8:16]], axis=0)
        best = v1[0:1] + v2[0:1]
        zsum = jnp.zeros_like(best)
        for _ in range(PEER_TOPK):
            thr = jnp.max(cand, axis=0, keepdims=True)
            cand = jnp.where(cand == thr, -jnp.inf, cand)
            zsum += jnp.exp(thr - best)
        plen = jnp.zeros_like(v1)
        for b in range(PEER_TOPK):
            plen += jnp.where(v1 + v2[b:b + 1] >= thr, 1.0, 0.0)
        l1 = jnp.zeros_like(s1)
        for a in range(PEER_TOPK):
            l1 = jnp.where(rank1 == float(a), plen[a:a + 1], l1)
        e1_ref[h, :, lanes] = jnp.exp(s1 - v1[0:1])
        l1_ref[h, :, lanes] = l1
        e2_ref[h, :, lanes] = jnp.exp(s2 - v2[0:1]) / zsum
        r2_ref[h, :, lanes] = rank2

    def body(t, carry):
        one(t // (tm // LANES), (t % (tm // LANES)) * LANES)
        return carry

    lax.fori_loop(0, PEER_HEADS * (tm // LANES), body, 0)


def _peer_gate(scores, *, tm):
    nhp, nk, n = scores.shape
    out = jax.ShapeDtypeStruct((nhp // 2, nk, n), F32)
    spec = pl.BlockSpec((nhp // 2, nk, tm), lambda i: (0, 0, i))
    return pl.pallas_call(
        _peer_gate_kernel,
        grid=(n // tm,),
        in_specs=[pl.BlockSpec((nhp, nk, tm), lambda i: (0, 0, i))],
        out_specs=[spec, spec, spec, spec],
        out_shape=[out, out, out, out],
        compiler_params=_cparams(("arbitrary",)),
        name="peer_gate",
    )(scores)


def _peer_dense_kernel(ft_ref, u_ref, vt_ref, e1_ref, l1_ref, e2_ref, r2_ref, o_ref, acc_ref, w_ref,
                       *, lane_chunk):
    j = pl.program_id(1)
    eb, t = w_ref.shape

    @pl.when(j == 0)
    def _():
        acc_ref[...] = jnp.zeros_like(acc_ref)

    hdn = jnp.dot(u_ref[...], ft_ref[...], preferred_element_type=F32)
    for a in range(eb // N_KEYS):
        i1 = j * (eb // N_KEYS) + a
        rows = slice(a * N_KEYS, (a + 1) * N_KEYS)
        for c0 in range(0, t, lane_chunk):
            cols = slice(c0, c0 + lane_chunk)
            gate = jnp.zeros((N_KEYS, lane_chunk), F32)
            for h in range(PEER_HEADS):
                e1 = e1_ref[h, pl.ds(i1, 1), cols]
                l1 = l1_ref[h, pl.ds(i1, 1), cols]
                gate += jnp.where(r2_ref[h, :, cols] < l1, e2_ref[h, :, cols] * e1, 0.0)
            x = hdn[rows, cols]
            gelu = 0.5 * x * (1.0 + lax.erf(x * (2.0 ** -0.5)))
            w_ref[rows, cols] = (gate * gelu).astype(BF16)
    acc_ref[...] += jnp.dot(vt_ref[...], w_ref[...], preferred_element_type=F32)

    @pl.when(j == pl.num_programs(1) - 1)
    def _():
        o_ref[...] = acc_ref[...].T


def _peer_dense(f_t, u_bf, v_t, e1, l1, e2, r2, *, t, eb, lane_chunk):
    d, n = f_t.shape
    ne = u_bf.shape[0]
    gspec = pl.BlockSpec((PEER_HEADS, N_KEYS, t), lambda i, j: (0, 0, i))
    kern = functools.partial(_peer_dense_kernel, lane_chunk=lane_chunk)
    return pl.pallas_call(
        kern,
        grid=(n // t, ne // eb),
        in_specs=[pl.BlockSpec((d, t), lambda i, j: (0, i)),
                  pl.BlockSpec((eb, d), lambda i, j: (j, 0)),
                  pl.BlockSpec((d, eb), lambda i, j: (0, j)),
                  gspec, gspec, gspec, gspec],
        out_specs=pl.BlockSpec((t, d), lambda i, j: (i, 0)),
        out_shape=jax.ShapeDtypeStruct((n, d), F32),
        scratch_shapes=[pltpu.VMEM((d, t), F32), pltpu.VMEM((eb, t), BF16)],
        compiler_params=_cparams(("arbitrary", "arbitrary")),
        name="peer_dense",
    )(f_t, u_bf, v_t, e1, l1, e2, r2)


def _ple_kernel(h1_ref, peer_ref, p_ref, pn_ref, wg_ref, wp_ref, fin_ref, o_ref, *, last_layer):
    h2 = h1_ref[...] + peer_ref[...]
    gate = jax.nn.sigmoid(jnp.dot(_rms(h2, pn_ref[...]).astype(BF16), wg_ref[...],
                                  preferred_element_type=F32))
    proj = jnp.dot(p_ref[...].astype(BF16), wp_ref[...], preferred_element_type=F32)
    h3 = h2 + proj * gate
    o_ref[...] = _rms(h3, fin_ref[...]) if last_layer else h3


def _ple_final(h1, peer, p2, ple_norm, w_gate, w_proj, final_norm, *, tm, last_layer):
    n, d = h1.shape
    tok = lambda w: pl.BlockSpec((tm, w), lambda i: (i, 0))
    return pl.pallas_call(
        functools.partial(_ple_kernel, last_layer=last_layer),
        grid=(n // tm,),
        in_specs=[tok(d), tok(d), tok(p2.shape[1]), _resident((1, d)), _resident(w_gate.shape),
                  _resident(w_proj.shape), _resident((1, d))],
        out_specs=tok(d),
        out_shape=jax.ShapeDtypeStruct((n, d), F32),
        compiler_params=_cparams(("arbitrary",)),
        name="ple_final",
    )(h1, peer, p2, ple_norm, w_gate, w_proj, final_norm)


def _swap_halves(w):
    half = w.shape[-1] // 2
    return jnp.concatenate([w[..., half:], w[..., :half]], axis=-1)


def _tile(n, pref):
    t = min(n, pref)
    assert n % t == 0
    return t


def kernel(x, p, positions, attn_norm, w_in, conv_w, q_norm, w_uq, kv_norm, w_ukv, conv_out_norm,
           attn_out_norm, w_out, ffn_norm, w_pq, sub_keys, u_tab, v_tab, ple_norm, w_ple_gate,
           w_ple_proj, final_norm):
    batch, seq, d = x.shape
    n = batch * seq
    depth = w_in.shape[0]
    row = lambda g: g.reshape(1, -1)

    inv_freq = ROPE_THETA ** (-(jnp.arange(0, QK_ROPE, 2, dtype=F32) / QK_ROPE))
    zeros = jnp.zeros((LANES - QK_ROPE,), F32)
    freq_row = row(jnp.concatenate([inv_freq, inv_freq, zeros]))
    sign_row = row(jnp.concatenate([-jnp.ones_like(inv_freq), jnp.ones_like(inv_freq), zeros]))
    pos2 = positions.reshape(n, 1)

    h = x.reshape(n, d)
    for i in range(depth):
        kr_cols = w_in[i][:, -QK_ROPE:]
        w_in_ext = jnp.concatenate([w_in[i], _swap_halves(kr_cols)], axis=-1).astype(BF16)
        uq = w_uq[i].reshape(-1, N_HEADS, QK_NOPE + QK_ROPE)
        uq_rope = uq[..., QK_NOPE:]
        w_uq_ext = jnp.concatenate([uq, _swap_halves(uq_rope)], axis=-1)
        w_uq_ext = w_uq_ext.reshape(-1, N_HEADS * HEAD_PAD).astype(BF16)
        ukv = w_ukv[i].reshape(-1, N_HEADS, QK_NOPE + V_HEAD)
        w_ukv_p = jnp.concatenate([ukv[..., :QK_NOPE].reshape(-1, N_HEADS * QK_NOPE),
                                   ukv[..., QK_NOPE:].reshape(-1, N_HEADS * V_HEAD)],
                                  axis=-1).astype(BF16)
        keys = sub_keys[i].reshape(-1, N_KEYS, sub_keys.shape[-1]).astype(BF16)

        conv_n, q_pad, k_pad, v = _in_proj(
            h, pos2, row(attn_norm[i]), w_in_ext, conv_w[i], row(q_norm[i]), w_uq_ext,
            row(kv_norm[i]), w_ukv_p, row(conv_out_norm[i]), freq_row, sign_row,
            seq=seq, tm=_tile(seq, 256))
        attn_n = _attention(q_pad, k_pad, v, row(attn_out_norm[i]), batch=batch, seq=seq,
                            tq=_tile(seq, 1024), tk=_tile(seq, 512))
        h1, f_t, scores = _out_proj(conv_n, attn_n, h, w_out[i].astype(BF16), row(ffn_norm[i]),
                                    w_pq[i].T.astype(BF16), keys, tm=_tile(n, 512))
        e1, l1, e2, r2 = _peer_gate(scores, tm=_tile(n, 512))
        peer = _peer_dense(f_t, u_tab[i].astype(BF16), v_tab[i].T.astype(BF16), e1, l1, e2, r2,
                           t=_tile(n, 512), eb=512, lane_chunk=256)
        h = _ple_final(h1, peer, p[i].reshape(n, -1), row(ple_norm[i]),
                       w_ple_gate[i].astype(BF16), w_ple_proj[i].astype(BF16),
                       row(final_norm), tm=_tile(n, 512), last_layer=i == depth - 1)
    return h.reshape(batch, seq, d)
```

```python
import functools

import jax
import jax.numpy as jnp
from jax import lax
from jax.experimental import pallas as pl
from jax.experimental.pallas import tpu as pltpu

EPS = 1e-6
ROPE_THETA = 10000.0
CONV_GROUP_DIM = 128
N_HEADS = 8
QK_NOPE = 128
QK_ROPE = 64
V_HEAD = 128
HEAD_PAD = 256
PEER_HEADS = 8
PEER_TOPK = 16
N_KEYS = 128
LANES = 128
NEG_BIG = -1e30

VMEM_LIMIT = 56 * 1024 * 1024

F32 = jnp.float32
BF16 = jnp.bfloat16


def _cparams(semantics):
    return pltpu.CompilerParams(dimension_semantics=semantics, vmem_limit_bytes=VMEM_LIMIT)


def _resident(shape):
    nd = len(shape)
    return pl.BlockSpec(shape, lambda *_: (0,) * nd, pipeline_mode=pl.Buffered(1))


def _rms(xf, g):
    return xf * lax.rsqrt(jnp.mean(xf * xf, axis=-1, keepdims=True) + EPS) * g


def _in_proj_kernel(x_ref, pos_ref, an_ref, win_ref, cw_ref, qn_ref, wuq_ref, kvn_ref, wukv_ref,
                    con_ref, freq_ref, sign_ref,
                    conv_ref, q_ref, k_ref, v_ref, carry_ref, *, tiles_per_seq, conv_ch, q_lora,
                    kv_lora):
    i = pl.program_id(0)
    tm = x_ref.shape[0]

    @pl.when(i % tiles_per_seq == 0)
    def _():
        carry_ref[...] = jnp.zeros_like(carry_ref)

    a = _rms(x_ref[...], an_ref[...]).astype(BF16)

    xin = jnp.dot(a, win_ref[:, 0:conv_ch], preferred_element_type=F32)
    c_g = jnp.dot(a, win_ref[:, 2 * conv_ch:3 * conv_ch], preferred_element_type=F32)
    u = c_g * xin
    prev = carry_ref[...]
    row = lax.broadcasted_iota(jnp.int32, u.shape, 0)
    u1 = jnp.where(row == 0, prev[7:8], pltpu.roll(u, 1, 0))
    u2 = jnp.where(row == 0, prev[6:7], jnp.where(row == 1, prev[7:8], pltpu.roll(u, 2, 0)))
    carry_ref[...] = u[tm - 8:tm]
    y = cw_ref[2:3] * u + cw_ref[1:2] * u1 + cw_ref[0:1] * u2
    b_g = jnp.dot(a, win_ref[:, conv_ch:2 * conv_ch], preferred_element_type=F32)
    conv_o = b_g * y
    for g in range(conv_ch // CONV_GROUP_DIM):
        sl = slice(g * CONV_GROUP_DIM, (g + 1) * CONV_GROUP_DIM)
        conv_ref[:, sl] = _rms(conv_o[:, sl], con_ref[:, sl]).astype(BF16)

    m0 = 3 * conv_ch
    lat = jnp.dot(a, win_ref[:, m0:], preferred_element_type=F32)
    ang = pos_ref[...].astype(F32) * freq_ref[...]
    cos_t = jnp.cos(ang) * jnp.abs(sign_ref[...])
    sin_t = jnp.sin(ang) * sign_ref[...]

    def rope(blk):
        return blk * cos_t + pltpu.roll(blk, QK_ROPE, 1) * sin_t

    cq = _rms(lat[:, 0:q_lora], qn_ref[...]).astype(BF16)
    qf = jnp.dot(cq, wuq_ref[...], preferred_element_type=F32)
    ckv = _rms(lat[:, q_lora:q_lora + kv_lora], kvn_ref[...]).astype(BF16)
    kvf = jnp.dot(ckv, wukv_ref[...], preferred_element_type=F32)
    kr = rope(lat[:, q_lora + kv_lora:]).astype(BF16)
    for h in range(N_HEADS):
        o = h * HEAD_PAD
        q_ref[:, o:o + QK_NOPE] = qf[:, o:o + QK_NOPE].astype(BF16)
        q_ref[:, o + QK_NOPE:o + HEAD_PAD] = rope(qf[:, o + QK_NOPE:o + HEAD_PAD]).astype(BF16)
        k_ref[:, o:o + QK_NOPE] = kvf[:, h * QK_NOPE:(h + 1) * QK_NOPE].astype(BF16)
        k_ref[:, o + QK_NOPE:o + HEAD_PAD] = kr
    v_ref[...] = kvf[:, N_HEADS * QK_NOPE:].astype(BF16)


def _in_proj(x2, pos2, attn_norm, w_in_ext, conv_w, q_norm, w_uq_ext, kv_norm, w_ukv_p,
             conv_out_norm, freq_row, sign_row, *, seq, tm):
    n, d = x2.shape
    conv_ch = conv_w.shape[-1]
    q_lora, kv_lora = q_norm.shape[-1], kv_norm.shape[-1]
    assert seq % tm == 0 and n % seq == 0
    tok = lambda w: pl.BlockSpec((tm, w), lambda i: (i, 0))
    kern = functools.partial(_in_proj_kernel, tiles_per_seq=seq // tm, conv_ch=conv_ch,
                             q_lora=q_lora, kv_lora=kv_lora)
    return pl.pallas_call(
        kern,
        grid=(n // tm,),
        in_specs=[tok(d), tok(1), _resident((1, d)), _resident(w_in_ext.shape),
                  _resident(conv_w.shape), _resident((1, q_lora)), _resident(w_uq_ext.shape),
                  _resident((1, kv_lora)), _resident(w_ukv_p.shape), _resident((1, conv_ch)),
                  _resident((1, LANES)), _resident((1, LANES))],
        out_specs=[tok(conv_ch), tok(N_HEADS * HEAD_PAD), tok(N_HEADS * HEAD_PAD),
                   tok(N_HEADS * V_HEAD)],
        out_shape=[jax.ShapeDtypeStruct((n, conv_ch), BF16),
                   jax.ShapeDtypeStruct((n, N_HEADS * HEAD_PAD), BF16),
                   jax.ShapeDtypeStruct((n, N_HEADS * HEAD_PAD), BF16),
                   jax.ShapeDtypeStruct((n, N_HEADS * V_HEAD), BF16)],
        scratch_shapes=[pltpu.VMEM((8, conv_ch), F32)],
        compiler_params=_cparams(("arbitrary",)),
        name="in_proj",
    )(x2, pos2, attn_norm, w_in_ext, conv_w, q_norm, w_uq_ext, kv_norm, w_ukv_p, conv_out_norm,
      freq_row, sign_row)


def _attn_kernel(q_ref, k_ref, v_ref, g_ref, o_ref, m_ref, l_ref, acc_ref, *, tk, rc, coef):
    qi = pl.program_id(2)
    tq = q_ref.shape[0]
    n_diag = tq // tk
    m_ref[...] = jnp.full_like(m_ref, NEG_BIG)
    l_ref[...] = jnp.zeros_like(l_ref)
    acc_ref[...] = jnp.zeros_like(acc_ref)

    def chunk_step(kb, r0, mask_offset):
        rows = slice(r0, r0 + rc)
        start = pl.multiple_of(kb * tk, tk)
        k = k_ref[pl.ds(start, tk), :]
        v = v_ref[pl.ds(start, tk), :]
        s = lax.dot_general(q_ref[rows, :], k, (((1,), (1,)), ((), ())),
                            preferred_element_type=F32)
        slabs = [s[:, j * LANES:(j + 1) * LANES] for j in range(tk // LANES)]
        if mask_offset is not None:
            r = lax.broadcasted_iota(jnp.int32, (rc, LANES), 0) + r0
            c = lax.broadcasted_iota(jnp.int32, (rc, LANES), 1)
            slabs = [jnp.where(c + (mask_offset + j * LANES) <= r, sl, NEG_BIG)
                     for j, sl in enumerate(slabs)]
        m_old = m_ref[rows, :]
        m_new = jnp.maximum(m_old, jnp.max(functools.reduce(jnp.maximum, slabs), axis=-1,
                                           keepdims=True))
        alpha = jnp.exp2((m_old - m_new) * coef)
        ps = [jnp.exp2((sl - m_new) * coef) for sl in slabs]
        l_ref[rows, :] = alpha * l_ref[rows, :] + jnp.sum(functools.reduce(jnp.add, ps), axis=-1,
                                                          keepdims=True)
        p = jnp.concatenate([x.astype(BF16) for x in ps], axis=1)
        acc_ref[rows, :] = alpha * acc_ref[rows, :] + jnp.dot(p, v, preferred_element_type=F32)
        m_ref[rows, :] = m_new

    n_full = qi * n_diag

    def body(kb, carry):
        for r0 in range(0, tq, rc):
            chunk_step(kb, r0, None)
        return carry

    lax.fori_loop(0, n_full, body, 0)
    for j in range(n_diag):
        for r0 in range(0, tq, rc):
            if r0 + rc - 1 < j * tk:
                continue
            needs_mask = r0 < j * tk + tk - 1
            chunk_step(n_full + j, r0, j * tk if needs_mask else None)

    o = acc_ref[...] / l_ref[...]
    o_ref[...] = _rms(o, g_ref[...]).astype(BF16)


def _attention(q_pad, k_pad, v, attn_out_norm, *, batch, seq, tq, tk, rc):
    n = q_pad.shape[0]
    assert seq % tq == 0 and tq % tk == 0 and tq % rc == 0 and tk % LANES == 0
    nq = seq // tq
    coef = float(QK_NOPE + QK_ROPE) ** -0.5 * 1.4426950408889634
    kern = functools.partial(_attn_kernel, tk=tk, rc=rc, coef=coef)
    return pl.pallas_call(
        kern,
        grid=(batch, N_HEADS, nq),
        in_specs=[pl.BlockSpec((tq, HEAD_PAD), lambda b, h, i: (b * nq + i, h)),
                  pl.BlockSpec((seq, HEAD_PAD), lambda b, h, i: (b, h)),
                  pl.BlockSpec((seq, V_HEAD), lambda b, h, i: (b, h)),
                  pl.BlockSpec((1, V_HEAD), lambda b, h, i: (0, h))],
        out_specs=pl.BlockSpec((tq, V_HEAD), lambda b, h, i: (b * nq + i, h)),
        out_shape=jax.ShapeDtypeStruct((n, N_HEADS * V_HEAD), BF16),
        scratch_shapes=[pltpu.VMEM((tq, LANES), F32), pltpu.VMEM((tq, LANES), F32),
                        pltpu.VMEM((tq, V_HEAD), F32)],
        compiler_params=_cparams(("arbitrary", "arbitrary", "arbitrary")),
        name="attention",
    )(q_pad, k_pad, v, attn_out_norm)


def _out_proj_kernel(conv_ref, attn_ref, x_ref, wo_ref, fn_ref, wpqt_ref, keys_ref,
                     h1_ref, ft_ref, sc_ref):
    conv_ch = conv_ref.shape[1]
    mix = jnp.dot(conv_ref[...], wo_ref[0:conv_ch, :], preferred_element_type=F32)
    mix += jnp.dot(attn_ref[...], wo_ref[conv_ch:, :], preferred_element_type=F32)
    h1 = x_ref[...] + mix
    h1_ref[...] = h1
    ft = _rms(h1, fn_ref[...]).T.astype(BF16)
    ft_ref[...] = ft
    qpt = jnp.dot(wpqt_ref[...], ft, preferred_element_type=F32).astype(BF16)
    for hp in range(keys_ref.shape[0]):
        sc_ref[hp] = jnp.dot(keys_ref[hp], qpt[hp * N_KEYS:(hp + 1) * N_KEYS, :],
                             preferred_element_type=F32)


def _out_proj(conv_n, attn_n, x2, w_out, ffn_norm, w_pq_t, keys, *, tm):
    n, d = x2.shape
    nhp = keys.shape[0]
    tok = lambda w: pl.BlockSpec((tm, w), lambda i: (i, 0))
    return pl.pallas_call(
        _out_proj_kernel,
        grid=(n // tm,),
        in_specs=[tok(conv_n.shape[1]), tok(attn_n.shape[1]), tok(d), _resident(w_out.shape),
                  _resident((1, d)), _resident(w_pq_t.shape), _resident(keys.shape)],
        out_specs=[tok(d), pl.BlockSpec((d, tm), lambda i: (0, i)),
                   pl.BlockSpec((nhp, N_KEYS, tm), lambda i: (0, 0, i))],
        out_shape=[jax.ShapeDtypeStruct((n, d), F32), jax.ShapeDtypeStruct((d, n), BF16),
                   jax.ShapeDtypeStruct((nhp, N_KEYS, n), F32)],
        compiler_params=_cparams(("arbitrary",)),
        name="out_proj",
    )(conv_n, attn_n, x2, w_out, ffn_norm, w_pq_t, keys)


def _top16(s):
    c = s.shape[1]
    rows = lax.broadcasted_iota(jnp.int32, (PEER_TOPK, c), 0)
    rank = jnp.full(s.shape, float(PEER_TOPK), F32)
    vals = jnp.zeros((PEER_TOPK, c), F32)
    x = s
    for k in range(PEER_TOPK):
        m = jnp.max(x, axis=0, keepdims=True)
        hit = x == m
        rank = jnp.where(hit, float(k), rank)
        x = jnp.where(hit, -jnp.inf, x)
        vals = jnp.where(rows == k, m, vals)
    return rank, vals


def _peer_gate_kernel(sc_ref, e1_ref, l1_ref, e2_ref, r2_ref):
    tm = sc_ref.shape[2]

    def one(h, c0):
        lanes = pl.ds(pl.multiple_of(c0, LANES), LANES)
        s1 = sc_ref[2 * h, :, lanes]
        s2 = sc_ref[2 * h + 1, :, lanes]
        rank1, v1 = _top16(s1)
        rank2, v2 = _top16(s2)
        cand = jnp.concatenate([v1 + v2[0:1]] + [v1[0:8] + v2[b:b + 1] for b in range(1, 8)]
                               + [v1[0:1] + v2[8:16]], axis=0)
        best = v1[0:1] + v2[0:1]
        zsum = jnp.zeros_like(best)
        for _ in range(PEER_TOPK):
            thr = jnp.max(cand, axis=0, keepdims=True)
            cand = jnp.where(cand == thr, -jnp.inf, cand)
            zsum += jnp.exp(thr - best)
        plen = jnp.zeros_like(v1)
        for b in range(PEER_TOPK):
            plen += jnp.where(v1 + v2[b:b + 1] >= thr, 1.0, 0.0)
        l1 = jnp.zeros_like(s1)
        for a in range(PEER_TOPK):
            l1 = jnp.where(rank1 == float(a), plen[a:a + 1], l1)
        e1_ref[h, :, lanes] = jnp.exp(s1 - v1[0:1])
        l1_ref[h, :, lanes] = l1
        e2_ref[h, :, lanes] = jnp.exp(s2 - v2[0:1]) / zsum
        r2_ref[h, :, lanes] = rank2

    def body(t, carry):
        one(t // (tm // LANES), (t % (tm // LANES)) * LANES)
        return carry

    lax.fori_loop(0, PEER_HEADS * (tm // LANES), body, 0)


def _peer_gate(scores, *, tm):
    nhp, nk, n = scores.shape
    out = jax.ShapeDtypeStruct((nhp // 2, nk, n), F32)
    spec = pl.BlockSpec((nhp // 2, nk, tm), lambda i: (0, 0, i))
    return pl.pallas_call(
        _peer_gate_kernel,
        grid=(n // tm,),
        in_specs=[pl.BlockSpec((nhp, nk, tm), lambda i: (0, 0, i))],
        out_specs=[spec, spec, spec, spec],
        out_shape=[out, out, out, out],
        compiler_params=_cparams(("arbitrary",)),
        name="peer_gate",
    )(scores)


def _peer_dense_kernel(ft_ref, u_ref, vtp_ref, vta_ref, e1_ref, l1_ref, e2_ref, r2_ref, o_ref,
                       hprev_ref, wa_ref, wb_ref, *, n_blocks):
    j = pl.program_id(1)
    eb, t = wa_ref.shape
    per_block = eb // N_KEYS

    @pl.when(j == 0)
    def _():
        o_ref[...] = jnp.zeros_like(o_ref)
        hprev_ref[...] = jnp.zeros_like(hprev_ref)

    def gated(hdn, blk, valid, w_ref):
        blk = jnp.clip(blk, 0, n_blocks - 1)
        live = jnp.where(valid, 1.0, 0.0)
        for a in range(per_block):
            i1 = blk * per_block + a
            rows = slice(a * N_KEYS, (a + 1) * N_KEYS)
            e1_rows = [e1_ref[h, pl.ds(i1, 1), :] for h in range(PEER_HEADS)]
            l1_rows = [l1_ref[h, pl.ds(i1, 1), :] * live for h in range(PEER_HEADS)]
            for c0 in range(0, t, LANES):
                cols = slice(c0, c0 + LANES)
                gate = jnp.zeros((N_KEYS, LANES), F32)
                for h in range(PEER_HEADS):
                    gate += jnp.where(r2_ref[h, :, cols] < l1_rows[h][:, cols],
                                      e2_ref[h, :, cols] * e1_rows[h][:, cols], 0.0)
                x = hdn[rows, cols]
                gelu = 0.5 * x * (1.0 + lax.erf(x * (2.0 ** -0.5)))
                w_ref[rows, cols] = (gate * gelu).astype(BF16)

    ft = ft_ref[...]
    hdn_a = jnp.dot(u_ref[0:eb, :], ft, preferred_element_type=F32)
    gated(hprev_ref, 2 * j - 1, j > 0, wb_ref)
    o_ref[...] += jnp.dot(vtp_ref[...], wb_ref[...], preferred_element_type=F32)
    hdn_b = jnp.dot(u_ref[eb:2 * eb, :], ft, preferred_element_type=F32)
    gated(hdn_a, 2 * j, 2 * j < n_blocks, wa_ref)
    o_ref[...] += jnp.dot(vta_ref[...], wa_ref[...], preferred_element_type=F32)
    hprev_ref[...] = hdn_b


def _peer_dense(f_t, u_bf, v_t, e1, l1, e2, r2, *, t, eb):
    d, n = f_t.shape
    ne = u_bf.shape[0]
    assert ne % (2 * eb) == 0 and eb % N_KEYS == 0
    n_blocks = ne // eb
    n_steps = n_blocks // 2 + 1
    gspec = pl.BlockSpec((PEER_HEADS, N_KEYS, t), lambda i, j: (0, 0, i))
    kern = functools.partial(_peer_dense_kernel, n_blocks=n_blocks)
    return pl.pallas_call(
        kern,
        grid=(n // t, n_steps),
        in_specs=[pl.BlockSpec((d, t), lambda i, j: (0, i)),
                  pl.BlockSpec((2 * eb, d), lambda i, j: (jnp.minimum(j, n_steps - 2), 0)),
                  pl.BlockSpec((d, eb), lambda i, j: (0, jnp.maximum(2 * j - 1, 0))),
                  pl.BlockSpec((d, eb), lambda i, j: (0, jnp.minimum(2 * j, n_blocks - 1))),
                  gspec, gspec, gspec, gspec],
        out_specs=pl.BlockSpec((d, t), lambda i, j: (0, i)),
        out_shape=jax.ShapeDtypeStruct((d, n), F32),
        scratch_shapes=[pltpu.VMEM((eb, t), F32), pltpu.VMEM((eb, t), BF16),
                        pltpu.VMEM((eb, t), BF16)],
        compiler_params=_cparams(("arbitrary", "arbitrary")),
        name="peer_dense",
    )(f_t, u_bf, v_t, v_t, e1, l1, e2, r2)


def _ple_kernel(h1_ref, peer_ref, p_ref, pn_ref, wg_ref, wp_ref, fin_ref, o_ref, *, last_layer):
    h2 = h1_ref[...] + peer_ref[...].T
    gate = jax.nn.sigmoid(jnp.dot(_rms(h2, pn_ref[...]).astype(BF16), wg_ref[...],
                                  preferred_element_type=F32))
    proj = jnp.dot(p_ref[...].astype(BF16), wp_ref[...], preferred_element_type=F32)
    h3 = h2 + proj * gate
    o_ref[...] = _rms(h3, fin_ref[...]) if last_layer else h3


def _ple_final(h1, peer, p2, ple_norm, w_gate, w_proj, final_norm, *, tm, last_layer):
    n, d = h1.shape
    tok = lambda w: pl.BlockSpec((tm, w), lambda i: (i, 0))
    return pl.pallas_call(
        functools.partial(_ple_kernel, last_layer=last_layer),
        grid=(n // tm,),
        in_specs=[tok(d), pl.BlockSpec((d, tm), lambda i: (0, i)), tok(p2.shape[1]),
                  _resident((1, d)), _resident(w_gate.shape),
                  _resident(w_proj.shape), _resident((1, d))],
        out_specs=tok(d),
        out_shape=jax.ShapeDtypeStruct((n, d), F32),
        compiler_params=_cparams(("arbitrary",)),
        name="ple_final",
    )(h1, peer, p2, ple_norm, w_gate, w_proj, final_norm)


def _swap_halves(w):
    half = w.shape[-1] // 2
    return jnp.concatenate([w[..., half:], w[..., :half]], axis=-1)


def _tile(n, pref):
    t = min(n, pref)
    assert n % t == 0
    return t


def kernel(x, p, positions, attn_norm, w_in, conv_w, q_norm, w_uq, kv_norm, w_ukv, conv_out_norm,
           attn_out_norm, w_out, ffn_norm, w_pq, sub_keys, u_tab, v_tab, ple_norm, w_ple_gate,
           w_ple_proj, final_norm):
    batch, seq, d = x.shape
    n = batch * seq
    depth = w_in.shape[0]
    row = lambda g: g.reshape(1, -1)

    inv_freq = ROPE_THETA ** (-(jnp.arange(0, QK_ROPE, 2, dtype=F32) / QK_ROPE))
    zeros = jnp.zeros((LANES - QK_ROPE,), F32)
    freq_row = row(jnp.concatenate([inv_freq, inv_freq, zeros]))
    sign_row = row(jnp.concatenate([-jnp.ones_like(inv_freq), jnp.ones_like(inv_freq), zeros]))
    pos2 = positions.reshape(n, 1)

    h = x.reshape(n, d)
    for i in range(depth):
        kr_cols = w_in[i][:, -QK_ROPE:]
        w_in_ext = jnp.concatenate([w_in[i], _swap_halves(kr_cols)], axis=-1).astype(BF16)
        uq = w_uq[i].reshape(-1, N_HEADS, QK_NOPE + QK_ROPE)
        uq_rope = uq[..., QK_NOPE:]
        w_uq_ext = jnp.concatenate([uq, _swap_halves(uq_rope)], axis=-1)
        w_uq_ext = w_uq_ext.reshape(-1, N_HEADS * HEAD_PAD).astype(BF16)
        ukv = w_ukv[i].reshape(-1, N_HEADS, QK_NOPE + V_HEAD)
        w_ukv_p = jnp.concatenate([ukv[..., :QK_NOPE].reshape(-1, N_HEADS * QK_NOPE),
                                   ukv[..., QK_NOPE:].reshape(-1, N_HEADS * V_HEAD)],
                                  axis=-1).astype(BF16)
        keys = sub_keys[i].reshape(-1, N_KEYS, sub_keys.shape[-1]).astype(BF16)

        conv_n, q_pad, k_pad, v = _in_proj(
            h, pos2, row(attn_norm[i]), w_in_ext, conv_w[i], row(q_norm[i]), w_uq_ext,
            row(kv_norm[i]), w_ukv_p, row(conv_out_norm[i]), freq_row, sign_row,
            seq=seq, tm=_tile(seq, 256))
        attn_n = _attention(q_pad, k_pad, v, row(attn_out_norm[i]), batch=batch, seq=seq,
                            tq=_tile(seq, 2048), tk=_tile(seq, 512), rc=_tile(seq, 512))
        h1, f_t, scores = _out_proj(conv_n, attn_n, h, w_out[i].astype(BF16), row(ffn_norm[i]),
                                    w_pq[i].T.astype(BF16), keys, tm=_tile(n, 512))
        e1, l1, e2, r2 = _peer_gate(scores, tm=_tile(n, 512))
        peer = _peer_dense(f_t, u_tab[i].astype(BF16), v_tab[i].T.astype(BF16), e1, l1, e2, r2,
                           t=_tile(n, 512), eb=512)
        h = _ple_final(h1, peer, p[i].reshape(n, -1), row(ple_norm[i]),
                       w_ple_gate[i].astype(BF16), w_ple_proj[i].astype(BF16),
                       row(final_norm), tm=_tile(n, 512), last_layer=i == depth - 1)
    return h.reshape(batch, seq, d)
```

```python
import functools

import jax
import jax.numpy as jnp
from jax import lax
from jax.experimental import pallas as pl
from jax.experimental.pallas import tpu as pltpu

EPS = 1e-6
ROPE_THETA = 10000.0
CONV_GROUP_DIM = 128
N_HEADS = 8
QK_NOPE = 128
QK_ROPE = 64
V_HEAD = 128
HEAD_PAD = 256
PEER_HEADS = 8
PEER_TOPK = 16
N_KEYS = 128
LANES = 128
HIDDEN_K_PIECES = 4
OUT_ROW_PIECES = 16
NEG_BIG = -1e30

VMEM_LIMIT = 56 * 1024 * 1024

F32 = jnp.float32
BF16 = jnp.bfloat16


def _cparams(semantics):
    return pltpu.CompilerParams(dimension_semantics=semantics, vmem_limit_bytes=VMEM_LIMIT)


def _resident(shape):
    nd = len(shape)
    return pl.BlockSpec(shape, lambda *_: (0,) * nd, pipeline_mode=pl.Buffered(1))


def _rms(xf, g):
    return xf * lax.rsqrt(jnp.mean(xf * xf, axis=-1, keepdims=True) + EPS) * g


def _in_proj_kernel(x_ref, pos_ref, an_ref, win_ref, cw_ref, qn_ref, wuq_ref, kvn_ref, wukv_ref,
                    con_ref, freq_ref, sign_ref,
                    conv_ref, q_ref, kt_ref, v_ref, carry_ref, *, tiles_per_seq, conv_ch, q_lora,
                    kv_lora):
    i = pl.program_id(0)
    tm = x_ref.shape[0]

    @pl.when(i % tiles_per_seq == 0)
    def _():
        carry_ref[...] = jnp.zeros_like(carry_ref)

    a = _rms(x_ref[...], an_ref[...]).astype(BF16)

    xin = jnp.dot(a, win_ref[:, 0:conv_ch], preferred_element_type=F32)
    c_g = jnp.dot(a, win_ref[:, 2 * conv_ch:3 * conv_ch], preferred_element_type=F32)
    u = c_g * xin
    prev = carry_ref[...]
    row = lax.broadcasted_iota(jnp.int32, u.shape, 0)
    u1 = jnp.where(row == 0, prev[7:8], pltpu.roll(u, 1, 0))
    u2 = jnp.where(row == 0, prev[6:7], jnp.where(row == 1, prev[7:8], pltpu.roll(u, 2, 0)))
    carry_ref[...] = u[tm - 8:tm]
    y = cw_ref[2:3] * u + cw_ref[1:2] * u1 + cw_ref[0:1] * u2
    b_g = jnp.dot(a, win_ref[:, conv_ch:2 * conv_ch], preferred_element_type=F32)
    conv_o = b_g * y
    for g in range(conv_ch // CONV_GROUP_DIM):
        sl = slice(g * CONV_GROUP_DIM, (g + 1) * CONV_GROUP_DIM)
        conv_ref[:, sl] = _rms(conv_o[:, sl], con_ref[:, sl]).astype(BF16)

    m0 = 3 * conv_ch
    lat = jnp.dot(a, win_ref[:, m0:], preferred_element_type=F32)
    ang = pos_ref[...].astype(F32) * freq_ref[...]
    cos_t = jnp.cos(ang) * jnp.abs(sign_ref[...])
    sin_t = jnp.sin(ang) * sign_ref[...]

    def rope(blk):
        return blk * cos_t + pltpu.roll(blk, QK_ROPE, 1) * sin_t

    cq = _rms(lat[:, 0:q_lora], qn_ref[...]).astype(BF16)
    qf = jnp.dot(cq, wuq_ref[...], preferred_element_type=F32)
    ckv = _rms(lat[:, q_lora:q_lora + kv_lora], kvn_ref[...]).astype(BF16)
    kvf = jnp.dot(ckv, wukv_ref[...], preferred_element_type=F32)
    kr_t = rope(lat[:, q_lora + kv_lora:]).T.astype(BF16)
    for h in range(N_HEADS):
        o = h * HEAD_PAD
        q_ref[:, o:o + QK_NOPE] = qf[:, o:o + QK_NOPE].astype(BF16)
        q_ref[:, o + QK_NOPE:o + HEAD_PAD] = rope(qf[:, o + QK_NOPE:o + HEAD_PAD]).astype(BF16)
        kt_ref[o:o + QK_NOPE, :] = kvf[:, h * QK_NOPE:(h + 1) * QK_NOPE].T.astype(BF16)
        kt_ref[o + QK_NOPE:o + HEAD_PAD, :] = kr_t
    v_ref[...] = kvf[:, N_HEADS * QK_NOPE:].astype(BF16)


def _in_proj(x2, pos2, attn_norm, w_in_ext, conv_w, q_norm, w_uq_ext, kv_norm, w_ukv_p,
             conv_out_norm, freq_row, sign_row, *, seq, tm):
    n, d = x2.shape
    conv_ch = conv_w.shape[-1]
    q_lora, kv_lora = q_norm.shape[-1], kv_norm.shape[-1]
    assert seq % tm == 0 and n % seq == 0
    tok = lambda w: pl.BlockSpec((tm, w), lambda i: (i, 0))
    kern = functools.partial(_in_proj_kernel, tiles_per_seq=seq // tm, conv_ch=conv_ch,
                             q_lora=q_lora, kv_lora=kv_lora)
    return pl.pallas_call(
        kern,
        grid=(n // tm,),
        in_specs=[tok(d), tok(1), _resident((1, d)), _resident(w_in_ext.shape),
                  _resident(conv_w.shape), _resident((1, q_lora)), _resident(w_uq_ext.shape),
                  _resident((1, kv_lora)), _resident(w_ukv_p.shape), _resident((1, conv_ch)),
                  _resident((1, LANES)), _resident((1, LANES))],
        out_specs=[tok(conv_ch), tok(N_HEADS * HEAD_PAD),
                   pl.BlockSpec((N_HEADS * HEAD_PAD, tm), lambda i: (0, i)),
                   tok(N_HEADS * V_HEAD)],
        out_shape=[jax.ShapeDtypeStruct((n, conv_ch), BF16),
                   jax.ShapeDtypeStruct((n, N_HEADS * HEAD_PAD), BF16),
                   jax.ShapeDtypeStruct((N_HEADS * HEAD_PAD, n), BF16),
                   jax.ShapeDtypeStruct((n, N_HEADS * V_HEAD), BF16)],
        scratch_shapes=[pltpu.VMEM((8, conv_ch), F32)],
        compiler_params=_cparams(("arbitrary",)),
        name="in_proj",
    )(x2, pos2, attn_norm, w_in_ext, conv_w, q_norm, w_uq_ext, kv_norm, w_ukv_p, conv_out_norm,
      freq_row, sign_row)


def _attn_kernel(q_ref, kt_ref, v_ref, g_ref, o_ref, m_ref, l_ref, acc_ref, *, tk, rc, coef):
    qi = pl.program_id(2)
    tq = q_ref.shape[0]
    n_diag = tq // tk
    m_ref[...] = jnp.full_like(m_ref, NEG_BIG)
    l_ref[...] = jnp.zeros_like(l_ref)
    acc_ref[...] = jnp.zeros_like(acc_ref)

    def chunk_step(kb, r0, mask_offset):
        rows = slice(r0, r0 + rc)
        start = pl.multiple_of(kb * tk, tk)
        kt = kt_ref[:, pl.ds(start, tk)]
        v = v_ref[pl.ds(start, tk), :]
        s = jnp.dot(q_ref[rows, :], kt, preferred_element_type=F32)
        slabs = [s[:, j * LANES:(j + 1) * LANES] for j in range(tk // LANES)]
        if mask_offset is not None:
            r = lax.broadcasted_iota(jnp.int32, (rc, LANES), 0) + r0
            c = lax.broadcasted_iota(jnp.int32, (rc, LANES), 1)
            slabs = [jnp.where(c + (mask_offset + j * LANES) <= r, sl, NEG_BIG)
                     for j, sl in enumerate(slabs)]
        m_old = m_ref[rows, :]
        m_new = jnp.maximum(m_old, jnp.max(functools.reduce(jnp.maximum, slabs), axis=-1,
                                           keepdims=True))
        alpha = jnp.exp2((m_old - m_new) * coef)
        ps = [jnp.exp2((sl - m_new) * coef) for sl in slabs]
        l_ref[rows, :] = alpha * l_ref[rows, :] + jnp.sum(functools.reduce(jnp.add, ps), axis=-1,
                                                          keepdims=True)
        p = jnp.concatenate([x.astype(BF16) for x in ps], axis=1)
        acc_ref[rows, :] = alpha * acc_ref[rows, :] + jnp.dot(p, v, preferred_element_type=F32)
        m_ref[rows, :] = m_new

    n_full = qi * n_diag

    def body(kb, carry):
        for r0 in range(0, tq, rc):
            chunk_step(kb, r0, None)
        return carry

    lax.fori_loop(0, n_full, body, 0)
    for j in range(n_diag):
        for r0 in range(0, tq, rc):
            if r0 + rc - 1 < j * tk:
                continue
            needs_mask = r0 < j * tk + tk - 1
            chunk_step(n_full + j, r0, j * tk if needs_mask else None)

    o = acc_ref[...] / l_ref[...]
    o_ref[...] = _rms(o, g_ref[...]).astype(BF16)


def _attention(q_pad, kt_pad, v, attn_out_norm, *, batch, seq, tq, tk, rc):
    n = q_pad.shape[0]
    assert seq % tq == 0 and tq % tk == 0 and tq % rc == 0 and tk % LANES == 0
    nq = seq // tq
    coef = float(QK_NOPE + QK_ROPE) ** -0.5 * 1.4426950408889634
    kern = functools.partial(_attn_kernel, tk=tk, rc=rc, coef=coef)
    return pl.pallas_call(
        kern,
        grid=(batch, N_HEADS, nq),
        in_specs=[pl.BlockSpec((tq, HEAD_PAD), lambda b, h, i: (b * nq + i, h)),
                  pl.BlockSpec((HEAD_PAD, seq), lambda b, h, i: (h, b)),
                  pl.BlockSpec((seq, V_HEAD), lambda b, h, i: (b, h)),
                  pl.BlockSpec((1, V_HEAD), lambda b, h, i: (0, h))],
        out_specs=pl.BlockSpec((tq, V_HEAD), lambda b, h, i: (b * nq + i, h)),
        out_shape=jax.ShapeDtypeStruct((n, N_HEADS * V_HEAD), BF16),
        scratch_shapes=[pltpu.VMEM((tq, LANES), F32), pltpu.VMEM((tq, LANES), F32),
                        pltpu.VMEM((tq, V_HEAD), F32)],
        compiler_params=_cparams(("arbitrary", "arbitrary", "arbitrary")),
        name="attention",
    )(q_pad, kt_pad, v, attn_out_norm)


def _out_proj_kernel(conv_ref, attn_ref, x_ref, wo_ref, fn_ref, wpqt_ref, keys_ref,
                     h1_ref, ft_ref, sc_ref):
    conv_ch = conv_ref.shape[1]
    mix = jnp.dot(conv_ref[...], wo_ref[0:conv_ch, :], preferred_element_type=F32)
    mix += jnp.dot(attn_ref[...], wo_ref[conv_ch:, :], preferred_element_type=F32)
    h1 = x_ref[...] + mix
    h1_ref[...] = h1
    ft = _rms(h1, fn_ref[...]).T.astype(BF16)
    ft_ref[...] = ft
    qpt = jnp.dot(wpqt_ref[...], ft, preferred_element_type=F32).astype(BF16)
    for hp in range(keys_ref.shape[0]):
        sc = jnp.dot(keys_ref[hp], qpt[hp * N_KEYS:(hp + 1) * N_KEYS, :],
                     preferred_element_type=F32)
        for c in range(sc_ref.shape[1]):
            sc_ref[hp, c] = sc[:, c * LANES:(c + 1) * LANES]


def _out_proj(conv_n, attn_n, x2, w_out, ffn_norm, w_pq_t, keys, *, tm):
    n, d = x2.shape
    nhp = keys.shape[0]
    tok = lambda w: pl.BlockSpec((tm, w), lambda i: (i, 0))
    return pl.pallas_call(
        _out_proj_kernel,
        grid=(n // tm,),
        in_specs=[tok(conv_n.shape[1]), tok(attn_n.shape[1]), tok(d), _resident(w_out.shape),
                  _resident((1, d)), _resident(w_pq_t.shape), _resident(keys.shape)],
        out_specs=[tok(d), pl.BlockSpec((d, tm), lambda i: (0, i)),
                   pl.BlockSpec((nhp, tm // LANES, N_KEYS, LANES), lambda i: (0, i, 0, 0))],
        out_shape=[jax.ShapeDtypeStruct((n, d), F32), jax.ShapeDtypeStruct((d, n), BF16),
                   jax.ShapeDtypeStruct((nhp, n // LANES, N_KEYS, LANES), F32)],
        compiler_params=_cparams(("arbitrary",)),
        name="out_proj",
    )(conv_n, attn_n, x2, w_out, ffn_norm, w_pq_t, keys)


def _top16(s):
    c = s.shape[1]
    rows = lax.broadcasted_iota(jnp.int32, (PEER_TOPK, c), 0)
    vals = jnp.zeros((PEER_TOPK, c), F32)
    x = s
    for k in range(PEER_TOPK):
        m = jnp.max(x, axis=0, keepdims=True)
        x = jnp.where(x == m, -jnp.inf, x)
        vals = jnp.where(rows == k, m, vals)
    return vals


def _peer_gate_kernel(sc_ref, e1_ref, t2_ref, e2_ref):
    n_chunks = sc_ref.shape[1]

    def one(h, c):
        s1 = sc_ref[2 * h, c]
        s2 = sc_ref[2 * h + 1, c]
        v1 = _top16(s1)
        v2 = _top16(s2)
        cand = jnp.concatenate([v1 + v2[0:1]] + [v1[0:8] + v2[b:b + 1] for b in range(1, 8)]
                               + [v1[0:1] + v2[8:16]], axis=0)
        best = v1[0:1] + v2[0:1]
        zsum = jnp.zeros_like(best)
        for _ in range(PEER_TOPK):
            thr = jnp.max(cand, axis=0, keepdims=True)
            cand = jnp.where(cand == thr, -jnp.inf, cand)
            zsum += jnp.exp(thr - best)
        tsel = jnp.full_like(v1, jnp.inf)
        for b in range(PEER_TOPK):
            tsel = jnp.where(v1 + v2[b:b + 1] >= thr, v2[b:b + 1], tsel)
        t2 = jnp.full_like(s1, jnp.inf)
        for a in range(PEER_TOPK):
            t2 = jnp.where(s1 == v1[a:a + 1], tsel[a:a + 1], t2)
        e1_ref[h, c] = jnp.exp(s1 - v1[0:1])
        t2_ref[h, c] = t2
        e2_ref[h, c] = jnp.exp(s2 - v2[0:1]) / zsum

    def body(t, carry):
        one(t // n_chunks, t % n_chunks)
        return carry

    lax.fori_loop(0, PEER_HEADS * n_chunks, body, 0)


def _peer_gate(scores, *, tm):
    nhp, n_chunks, nk, _ = scores.shape
    cpt = tm // LANES
    out = [jax.ShapeDtypeStruct((nhp // 2, n_chunks, nk, LANES), F32)] * 3
    spec = pl.BlockSpec((nhp // 2, cpt, nk, LANES), lambda i: (0, i, 0, 0))
    return pl.pallas_call(
        _peer_gate_kernel,
        grid=(n_chunks // cpt,),
        in_specs=[pl.BlockSpec((nhp, cpt, nk, LANES), lambda i: (0, i, 0, 0))],
        out_specs=[spec, spec, spec],
        out_shape=out,
        compiler_params=_cparams(("arbitrary",)),
        name="peer_gate",
    )(scores)


def _peer_dense_kernel(ft_ref, u_ref, vt_ref, e1_ref, t2_ref, e2_ref, s2_ref, o_ref,
                       h0_ref, h1_ref, w0_ref, w1_ref, *, n_blocks):
    s = pl.program_id(1)
    n_chunks, eb, _ = w0_ref.shape
    per_block = eb // N_KEYS

    @pl.when(s == 0)
    def _():
        o_ref[...] = jnp.zeros_like(o_ref)
        for ref in (h0_ref, h1_ref, w0_ref, w1_ref):
            ref[...] = jnp.zeros_like(ref)

    def stage(h_new, h_old, w_new, w_old, second):
        blk = s - 1
        live = jnp.where((blk >= 0) & (blk < n_blocks), 1.0, 0.0)
        pair = jnp.clip(blk // 2, 0, n_blocks // 2 - 1)

        def hidden_piece(m0, c0, k0, nk):
            res = jnp.dot(u_ref[m0:m0 + eb // 2, k0:k0 + nk],
                          ft_ref[k0:k0 + nk, c0 * LANES:(c0 + 2) * LANES],
                          preferred_element_type=F32)
            if k0 == 0:
                h_new[c0, m0:m0 + eb // 2, :] = res[:, :LANES]
                h_new[c0 + 1, m0:m0 + eb // 2, :] = res[:, LANES:]
            else:
                h_new[c0, m0:m0 + eb // 2, :] += res[:, :LANES]
                h_new[c0 + 1, m0:m0 + eb // 2, :] += res[:, LANES:]

        def out_piece(r0, nr):
            w = jnp.concatenate([w_old[c] for c in range(n_chunks)], axis=1)
            o_ref[r0:r0 + nr, :] += jnp.dot(vt_ref[r0:r0 + nr, :], w, preferred_element_type=F32)

        def gate_tile(a, c):
            r = per_block * second + a
            rows = slice(a * N_KEYS, (a + 1) * N_KEYS)
            gate = jnp.zeros((N_KEYS, LANES), F32)
            for h in range(PEER_HEADS):
                t2 = t2_ref[h, c, pair, r:r + 1, :]
                e1 = e1_ref[h, c, pair, r:r + 1, :] * live
                gate += jnp.where(s2_ref[h, c] >= t2, e2_ref[h, c] * e1, 0.0)
            x = h_old[c, rows, :]
            gelu = 0.5 * x * (1.0 + lax.erf(x * (2.0 ** -0.5)))
            w_new[c, rows, :] = (gate * gelu).astype(BF16)

        d = o_ref.shape[0]
        nk = d // HIDDEN_K_PIECES
        nr = d // OUT_ROW_PIECES
        pieces = [functools.partial(hidden_piece, m0, c0, k0, nk) for k0 in range(0, d, nk)
                  for m0 in (0, eb // 2) for c0 in range(0, n_chunks, 2)]
        pieces += [functools.partial(out_piece, r0, nr) for r0 in range(0, d, nr)]
        tiles = [(a, c) for a in range(per_block) for c in range(n_chunks)]
        done = 0
        for q, piece in enumerate(pieces):
            piece()
            want = (q + 1) * len(tiles) // len(pieces)
            for a, c in tiles[done:want]:
                gate_tile(a, c)
            done = want

    @pl.when(s % 2 == 0)
    def _():
        stage(h0_ref, h1_ref, w1_ref, w0_ref, 1)

    @pl.when(s % 2 == 1)
    def _():
        stage(h1_ref, h0_ref, w0_ref, w1_ref, 0)


def _peer_dense(f_t, u_bf, v_t, e1, t2, e2, scores, *, t, eb):
    d, n = f_t.shape
    ne = u_bf.shape[0]
    assert ne % (2 * eb) == 0 and 2 * eb == 8 * N_KEYS
    assert t % (2 * LANES) == 0
    n_blocks = ne // eb
    cpt = t // LANES
    gspec = pl.BlockSpec((PEER_HEADS, cpt, N_KEYS, LANES), lambda i, s: (0, i, 0, 0))
    pspec = pl.BlockSpec((PEER_HEADS, cpt, N_KEYS // 8, 8, LANES), lambda i, s: (0, i, 0, 0, 0))
    s2spec = pl.BlockSpec((PEER_HEADS, None, cpt, N_KEYS, LANES), lambda i, s: (0, 1, i, 0, 0))
    e1 = e1.reshape(PEER_HEADS, n // LANES, N_KEYS // 8, 8, LANES)
    t2 = t2.reshape(PEER_HEADS, n // LANES, N_KEYS // 8, 8, LANES)
    scores = scores.reshape(PEER_HEADS, 2, n // LANES, N_KEYS, LANES)
    kern = functools.partial(_peer_dense_kernel, n_blocks=n_blocks)
    return pl.pallas_call(
        kern,
        grid=(n // t, n_blocks + 2),
        in_specs=[pl.BlockSpec((d, t), lambda i, s: (0, i)),
                  pl.BlockSpec((eb, d), lambda i, s: (jnp.minimum(s, n_blocks - 1), 0)),
                  pl.BlockSpec((None, d, eb), lambda i, s: (jnp.clip(s - 2, 0, n_blocks - 1), 0, 0)),
                  pspec, pspec, gspec, s2spec],
        out_specs=pl.BlockSpec((d, t), lambda i, s: (0, i)),
        out_shape=jax.ShapeDtypeStruct((d, n), F32),
        scratch_shapes=[pltpu.VMEM((cpt, eb, LANES), F32), pltpu.VMEM((cpt, eb, LANES), F32),
                        pltpu.VMEM((cpt, eb, LANES), BF16), pltpu.VMEM((cpt, eb, LANES), BF16)],
        compiler_params=_cparams(("arbitrary", "arbitrary")),
        name="peer_dense",
    )(f_t, u_bf, v_t, e1, t2, e2, scores)


def _ple_kernel(h1_ref, peer_ref, p_ref, pn_ref, wg_ref, wp_ref, fin_ref, o_ref, *, last_layer):
    h2 = h1_ref[...] + peer_ref[...].T
    gate = jax.nn.sigmoid(jnp.dot(_rms(h2, pn_ref[...]).astype(BF16), wg_ref[...],
                                  preferred_element_type=F32))
    proj = jnp.dot(p_ref[...].astype(BF16), wp_ref[...], preferred_element_type=F32)
    h3 = h2 + proj * gate
    o_ref[...] = _rms(h3, fin_ref[...]) if last_layer else h3


def _ple_final(h1, peer, p2, ple_norm, w_gate, w_proj, final_norm, *, tm, last_layer):
    n, d = h1.shape
    tok = lambda w: pl.BlockSpec((tm, w), lambda i: (i, 0))
    return pl.pallas_call(
        functools.partial(_ple_kernel, last_layer=last_layer),
        grid=(n // tm,),
        in_specs=[tok(d), pl.BlockSpec((d, tm), lambda i: (0, i)), tok(p2.shape[1]),
                  _resident((1, d)), _resident(w_gate.shape),
                  _resident(w_proj.shape), _resident((1, d))],
        out_specs=tok(d),
        out_shape=jax.ShapeDtypeStruct((n, d), F32),
        compiler_params=_cparams(("arbitrary",)),
        name="ple_final",
    )(h1, peer, p2, ple_norm, w_gate, w_proj, final_norm)


def _swap_halves(w):
    half = w.shape[-1] // 2
    return jnp.concatenate([w[..., half:], w[..., :half]], axis=-1)


def _tile(n, pref):
    t = min(n, pref)
    assert n % t == 0
    return t


def kernel(x, p, positions, attn_norm, w_in, conv_w, q_norm, w_uq, kv_norm, w_ukv, conv_out_norm,
           attn_out_norm, w_out, ffn_norm, w_pq, sub_keys, u_tab, v_tab, ple_norm, w_ple_gate,
           w_ple_proj, final_norm):
    batch, seq, d = x.shape
    n = batch * seq
    depth = w_in.shape[0]
    row = lambda g: g.reshape(1, -1)

    inv_freq = ROPE_THETA ** (-(jnp.arange(0, QK_ROPE, 2, dtype=F32) / QK_ROPE))
    zeros = jnp.zeros((LANES - QK_ROPE,), F32)
    freq_row = row(jnp.concatenate([inv_freq, inv_freq, zeros]))
    sign_row = row(jnp.concatenate([-jnp.ones_like(inv_freq), jnp.ones_like(inv_freq), zeros]))
    pos2 = positions.reshape(n, 1)

    h = x.reshape(n, d)
    for i in range(depth):
        kr_cols = w_in[i][:, -QK_ROPE:]
        w_in_ext = jnp.concatenate([w_in[i], _swap_halves(kr_cols)], axis=-1).astype(BF16)
        uq = w_uq[i].reshape(-1, N_HEADS, QK_NOPE + QK_ROPE)
        uq_rope = uq[..., QK_NOPE:]
        w_uq_ext = jnp.concatenate([uq, _swap_halves(uq_rope)], axis=-1)
        w_uq_ext = w_uq_ext.reshape(-1, N_HEADS * HEAD_PAD).astype(BF16)
        ukv = w_ukv[i].reshape(-1, N_HEADS, QK_NOPE + V_HEAD)
        w_ukv_p = jnp.concatenate([ukv[..., :QK_NOPE].reshape(-1, N_HEADS * QK_NOPE),
                                   ukv[..., QK_NOPE:].reshape(-1, N_HEADS * V_HEAD)],
                                  axis=-1).astype(BF16)
        keys = sub_keys[i].reshape(-1, N_KEYS, sub_keys.shape[-1]).astype(BF16)

        conv_n, q_pad, k_pad, v = _in_proj(
            h, pos2, row(attn_norm[i]), w_in_ext, conv_w[i], row(q_norm[i]), w_uq_ext,
            row(kv_norm[i]), w_ukv_p, row(conv_out_norm[i]), freq_row, sign_row,
            seq=seq, tm=_tile(seq, 512))
        attn_n = _attention(q_pad, k_pad, v, row(attn_out_norm[i]), batch=batch, seq=seq,
                            tq=_tile(seq, 2048), tk=_tile(seq, 512), rc=_tile(seq, 512))
        h1, f_t, scores = _out_proj(conv_n, attn_n, h, w_out[i].astype(BF16), row(ffn_norm[i]),
                                    w_pq[i].T.astype(BF16), keys, tm=_tile(n, 512))
        e1, t2, e2 = _peer_gate(scores, tm=_tile(n, 512))
        eb = 512
        v_blocks = v_tab[i].reshape(-1, eb, d).transpose(0, 2, 1).astype(BF16)
        peer = _peer_dense(f_t, u_tab[i].astype(BF16), v_blocks, e1, t2, e2, scores,
                           t=_tile(n, 512), eb=eb)
        h = _ple_final(h1, peer, p[i].reshape(n, -1), row(ple_norm[i]),
                       w_ple_gate[i].astype(BF16), w_ple_proj[i].astype(BF16),
                       row(final_norm), tm=_tile(n, 512), last_layer=i == depth - 1)
    return h.reshape(batch, seq, d)
```

```python
import functools

import jax
import jax.numpy as jnp
from jax import lax
from jax.experimental import pallas as pl
from jax.experimental.pallas import tpu as pltpu

EPS = 1e-6
ROPE_THETA = 10000.0
CONV_GROUP_DIM = 128
N_HEADS = 8
QK_NOPE = 128
QK_ROPE = 64
V_HEAD = 128
HEAD_PAD = 256
PEER_HEADS = 8
PEER_TOPK = 16
N_KEYS = 128
LANES = 128
HIDDEN_K_PIECES = 4
OUT_ROW_PIECES = 16
NEG_BIG = -1e30

VMEM_LIMIT = 56 * 1024 * 1024

F32 = jnp.float32
BF16 = jnp.bfloat16


def _cparams(semantics):
    return pltpu.CompilerParams(dimension_semantics=semantics, vmem_limit_bytes=VMEM_LIMIT)


def _resident(shape):
    nd = len(shape)
    return pl.BlockSpec(shape, lambda *_: (0,) * nd, pipeline_mode=pl.Buffered(1))


def _rms(xf, g):
    return xf * lax.rsqrt(jnp.mean(xf * xf, axis=-1, keepdims=True) + EPS) * g


def _in_proj_kernel(x_ref, pos_ref, an_ref, win_ref, cw_ref, qn_ref, wuq_ref, kvn_ref, wukv_ref,
                    con_ref, freq_ref, sign_ref,
                    conv_ref, q_ref, kt_ref, v_ref, carry_ref, *, tiles_per_seq, conv_ch, q_lora,
                    kv_lora):
    i = pl.program_id(0)
    tm = x_ref.shape[0]

    @pl.when(i % tiles_per_seq == 0)
    def _():
        carry_ref[...] = jnp.zeros_like(carry_ref)

    a = _rms(x_ref[...], an_ref[...]).astype(BF16)

    xin = jnp.dot(a, win_ref[:, 0:conv_ch], preferred_element_type=F32)
    c_g = jnp.dot(a, win_ref[:, 2 * conv_ch:3 * conv_ch], preferred_element_type=F32)
    u = c_g * xin
    prev = carry_ref[...]
    row = lax.broadcasted_iota(jnp.int32, u.shape, 0)
    u1 = jnp.where(row == 0, prev[7:8], pltpu.roll(u, 1, 0))
    u2 = jnp.where(row == 0, prev[6:7], jnp.where(row == 1, prev[7:8], pltpu.roll(u, 2, 0)))
    carry_ref[...] = u[tm - 8:tm]
    y = cw_ref[2:3] * u + cw_ref[1:2] * u1 + cw_ref[0:1] * u2
    b_g = jnp.dot(a, win_ref[:, conv_ch:2 * conv_ch], preferred_element_type=F32)
    conv_o = b_g * y
    for g in range(conv_ch // CONV_GROUP_DIM):
        sl = slice(g * CONV_GROUP_DIM, (g + 1) * CONV_GROUP_DIM)
        conv_ref[:, sl] = _rms(conv_o[:, sl], con_ref[:, sl]).astype(BF16)

    m0 = 3 * conv_ch
    lat = jnp.dot(a, win_ref[:, m0:], preferred_element_type=F32)
    ang = pos_ref[...].astype(F32) * freq_ref[...]
    cos_t = jnp.cos(ang) * jnp.abs(sign_ref[...])
    sin_t = jnp.sin(ang) * sign_ref[...]

    def rope(blk):
        return blk * cos_t + pltpu.roll(blk, QK_ROPE, 1) * sin_t

    cq = _rms(lat[:, 0:q_lora], qn_ref[...]).astype(BF16)
    qf = jnp.dot(cq, wuq_ref[...], preferred_element_type=F32)
    ckv = _rms(lat[:, q_lora:q_lora + kv_lora], kvn_ref[...]).astype(BF16)
    kvf = jnp.dot(ckv, wukv_ref[...], preferred_element_type=F32)
    kr_t = rope(lat[:, q_lora + kv_lora:]).T.astype(BF16)
    for h in range(N_HEADS):
        o = h * HEAD_PAD
        q_ref[:, o:o + QK_NOPE] = qf[:, o:o + QK_NOPE].astype(BF16)
        q_ref[:, o + QK_NOPE:o + HEAD_PAD] = rope(qf[:, o + QK_NOPE:o + HEAD_PAD]).astype(BF16)
        kt_ref[o:o + QK_NOPE, :] = kvf[:, h * QK_NOPE:(h + 1) * QK_NOPE].T.astype(BF16)
        kt_ref[o + QK_NOPE:o + HEAD_PAD, :] = kr_t
    v_ref[...] = kvf[:, N_HEADS * QK_NOPE:].astype(BF16)


def _in_proj(x2, pos2, attn_norm, w_in_ext, conv_w, q_norm, w_uq_ext, kv_norm, w_ukv_p,
             conv_out_norm, freq_row, sign_row, *, seq, tm):
    n, d = x2.shape
    conv_ch = conv_w.shape[-1]
    q_lora, kv_lora = q_norm.shape[-1], kv_norm.shape[-1]
    assert seq % tm == 0 and n % seq == 0
    tok = lambda w: pl.BlockSpec((tm, w), lambda i: (i, 0))
    kern = functools.partial(_in_proj_kernel, tiles_per_seq=seq // tm, conv_ch=conv_ch,
                             q_lora=q_lora, kv_lora=kv_lora)
    return pl.pallas_call(
        kern,
        grid=(n // tm,),
        in_specs=[tok(d), tok(1), _resident((1, d)), _resident(w_in_ext.shape),
                  _resident(conv_w.shape), _resident((1, q_lora)), _resident(w_uq_ext.shape),
                  _resident((1, kv_lora)), _resident(w_ukv_p.shape), _resident((1, conv_ch)),
                  _resident((1, LANES)), _resident((1, LANES))],
        out_specs=[tok(conv_ch), tok(N_HEADS * HEAD_PAD),
                   pl.BlockSpec((N_HEADS * HEAD_PAD, tm), lambda i: (0, i)),
                   tok(N_HEADS * V_HEAD)],
        out_shape=[jax.ShapeDtypeStruct((n, conv_ch), BF16),
                   jax.ShapeDtypeStruct((n, N_HEADS * HEAD_PAD), BF16),
                   jax.ShapeDtypeStruct((N_HEADS * HEAD_PAD, n), BF16),
                   jax.ShapeDtypeStruct((n, N_HEADS * V_HEAD), BF16)],
        scratch_shapes=[pltpu.VMEM((8, conv_ch), F32)],
        compiler_params=_cparams(("arbitrary",)),
        name="in_proj",
    )(x2, pos2, attn_norm, w_in_ext, conv_w, q_norm, w_uq_ext, kv_norm, w_ukv_p, conv_out_norm,
      freq_row, sign_row)


def _attn_kernel(q_ref, kt_ref, v_ref, g_ref, o_ref, m_ref, l_ref, acc_ref, *, tk, rc, coef):
    qi = pl.program_id(2)
    tq = q_ref.shape[0]
    n_diag = tq // tk
    m_ref[...] = jnp.full_like(m_ref, NEG_BIG)
    l_ref[...] = jnp.zeros_like(l_ref)
    acc_ref[...] = jnp.zeros_like(acc_ref)

    def chunk_step(kb, r0, mask_offset):
        rows = slice(r0, r0 + rc)
        start = pl.multiple_of(kb * tk, tk)
        kt = kt_ref[:, pl.ds(start, tk)]
        v = v_ref[pl.ds(start, tk), :]
        s = jnp.dot(q_ref[rows, :], kt, preferred_element_type=F32)
        slabs = [s[:, j * LANES:(j + 1) * LANES] for j in range(tk // LANES)]
        if mask_offset is not None:
            r = lax.broadcasted_iota(jnp.int32, (rc, LANES), 0) + r0
            c = lax.broadcasted_iota(jnp.int32, (rc, LANES), 1)
            slabs = [jnp.where(c + (mask_offset + j * LANES) <= r, sl, NEG_BIG)
                     for j, sl in enumerate(slabs)]
        m_old = m_ref[rows, :]
        m_new = jnp.maximum(m_old, jnp.max(functools.reduce(jnp.maximum, slabs), axis=-1,
                                           keepdims=True))
        alpha = jnp.exp2((m_old - m_new) * coef)
        ps = [jnp.exp2((sl - m_new) * coef) for sl in slabs]
        l_ref[rows, :] = alpha * l_ref[rows, :] + jnp.sum(functools.reduce(jnp.add, ps), axis=-1,
                                                          keepdims=True)
        p = jnp.concatenate([x.astype(BF16) for x in ps], axis=1)
        acc_ref[rows, :] = alpha * acc_ref[rows, :] + jnp.dot(p, v, preferred_element_type=F32)
        m_ref[rows, :] = m_new

    n_full = qi * n_diag

    def body(kb, carry):
        for r0 in range(0, tq, rc):
            chunk_step(kb, r0, None)
        return carry

    lax.fori_loop(0, n_full, body, 0)
    for j in range(n_diag):
        for r0 in range(0, tq, rc):
            if r0 + rc - 1 < j * tk:
                continue
            needs_mask = r0 < j * tk + tk - 1
            chunk_step(n_full + j, r0, j * tk if needs_mask else None)

    o = acc_ref[...] / l_ref[...]
    o_ref[...] = _rms(o, g_ref[...]).astype(BF16)


def _attention(q_pad, kt_pad, v, attn_out_norm, *, batch, seq, tq, tk, rc):
    n = q_pad.shape[0]
    assert seq % tq == 0 and tq % tk == 0 and tq % rc == 0 and tk % LANES == 0
    nq = seq // tq
    coef = float(QK_NOPE + QK_ROPE) ** -0.5 * 1.4426950408889634
    kern = functools.partial(_attn_kernel, tk=tk, rc=rc, coef=coef)
    return pl.pallas_call(
        kern,
        grid=(batch, N_HEADS, nq),
        in_specs=[pl.BlockSpec((tq, HEAD_PAD), lambda b, h, i: (b * nq + i, h)),
                  pl.BlockSpec((HEAD_PAD, seq), lambda b, h, i: (h, b)),
                  pl.BlockSpec((seq, V_HEAD), lambda b, h, i: (b, h)),
                  pl.BlockSpec((1, V_HEAD), lambda b, h, i: (0, h))],
        out_specs=pl.BlockSpec((tq, V_HEAD), lambda b, h, i: (b * nq + i, h)),
        out_shape=jax.ShapeDtypeStruct((n, N_HEADS * V_HEAD), BF16),
        scratch_shapes=[pltpu.VMEM((tq, LANES), F32), pltpu.VMEM((tq, LANES), F32),
                        pltpu.VMEM((tq, V_HEAD), F32)],
        compiler_params=_cparams(("arbitrary", "arbitrary", "arbitrary")),
        name="attention",
    )(q_pad, kt_pad, v, attn_out_norm)


def _out_proj_kernel(conv_ref, attn_ref, x_ref, wo_ref, fn_ref, wpqt_ref, keys_ref,
                     h1_ref, ft_ref, sc_ref):
    conv_ch = conv_ref.shape[1]
    mix = jnp.dot(conv_ref[...], wo_ref[0:conv_ch, :], preferred_element_type=F32)
    mix += jnp.dot(attn_ref[...], wo_ref[conv_ch:, :], preferred_element_type=F32)
    h1 = x_ref[...] + mix
    h1_ref[...] = h1
    ft = _rms(h1, fn_ref[...]).T.astype(BF16)
    ft_ref[...] = ft
    qpt = jnp.dot(wpqt_ref[...], ft, preferred_element_type=F32).astype(BF16)
    for hp in range(keys_ref.shape[0]):
        sc = jnp.dot(keys_ref[hp], qpt[hp * N_KEYS:(hp + 1) * N_KEYS, :],
                     preferred_element_type=F32)
        for c in range(sc_ref.shape[1]):
            sc_ref[hp, c] = sc[:, c * LANES:(c + 1) * LANES]


def _out_proj(conv_n, attn_n, x2, w_out, ffn_norm, w_pq_t, keys, *, tm):
    n, d = x2.shape
    nhp = keys.shape[0]
    tok = lambda w: pl.BlockSpec((tm, w), lambda i: (i, 0))
    return pl.pallas_call(
        _out_proj_kernel,
        grid=(n // tm,),
        in_specs=[tok(conv_n.shape[1]), tok(attn_n.shape[1]), tok(d), _resident(w_out.shape),
                  _resident((1, d)), _resident(w_pq_t.shape), _resident(keys.shape)],
        out_specs=[tok(d), pl.BlockSpec((d, tm), lambda i: (0, i)),
                   pl.BlockSpec((nhp, tm // LANES, N_KEYS, LANES), lambda i: (0, i, 0, 0))],
        out_shape=[jax.ShapeDtypeStruct((n, d), F32), jax.ShapeDtypeStruct((d, n), BF16),
                   jax.ShapeDtypeStruct((nhp, n // LANES, N_KEYS, LANES), F32)],
        compiler_params=_cparams(("arbitrary",)),
        name="out_proj",
    )(conv_n, attn_n, x2, w_out, ffn_norm, w_pq_t, keys)


def _top16(s):
    c = s.shape[1]
    rows = lax.broadcasted_iota(jnp.int32, (PEER_TOPK, c), 0)
    vals = jnp.zeros((PEER_TOPK, c), F32)
    x = s
    for k in range(PEER_TOPK):
        m = jnp.max(x, axis=0, keepdims=True)
        x = jnp.where(x == m, -jnp.inf, x)
        vals = jnp.where(rows == k, m, vals)
    return vals


def _peer_gate_kernel(sc_ref, e1_ref, t2_ref, e2_ref):
    n_chunks = sc_ref.shape[1]

    def one(h, c):
        s1 = sc_ref[2 * h, c]
        s2 = sc_ref[2 * h + 1, c]
        v1 = _top16(s1)
        v2 = _top16(s2)
        cand = jnp.concatenate([v1 + v2[0:1]] + [v1[0:8] + v2[b:b + 1] for b in range(1, 8)]
                               + [v1[0:1] + v2[8:16]], axis=0)
        best = v1[0:1] + v2[0:1]
        zsum = jnp.zeros_like(best)
        for _ in range(PEER_TOPK):
            thr = jnp.max(cand, axis=0, keepdims=True)
            cand = jnp.where(cand == thr, -jnp.inf, cand)
            zsum += jnp.exp(thr - best)
        tsel = jnp.full_like(v1, jnp.inf)
        for b in range(PEER_TOPK):
            tsel = jnp.where(v1 + v2[b:b + 1] >= thr, v2[b:b + 1], tsel)
        t2 = jnp.full_like(s1, jnp.inf)
        for a in range(PEER_TOPK):
            t2 = jnp.where(s1 == v1[a:a + 1], tsel[a:a + 1], t2)
        e1_ref[h, c] = jnp.exp(s1 - v1[0:1])
        t2_ref[h, c] = t2
        e2_ref[h, c] = jnp.exp(s2 - v2[0:1]) / zsum

    def body(t, carry):
        one(t // n_chunks, t % n_chunks)
        return carry

    lax.fori_loop(0, PEER_HEADS * n_chunks, body, 0)


def _peer_gate(scores, *, tm):
    nhp, n_chunks, nk, _ = scores.shape
    cpt = tm // LANES
    out = [jax.ShapeDtypeStruct((nhp // 2, n_chunks, nk, LANES), F32)] * 3
    spec = pl.BlockSpec((nhp // 2, cpt, nk, LANES), lambda i: (0, i, 0, 0))
    return pl.pallas_call(
        _peer_gate_kernel,
        grid=(n_chunks // cpt,),
        in_specs=[pl.BlockSpec((nhp, cpt, nk, LANES), lambda i: (0, i, 0, 0))],
        out_specs=[spec, spec, spec],
        out_shape=out,
        compiler_params=_cparams(("arbitrary",)),
        name="peer_gate",
    )(scores)


def _peer_dense_kernel(ft_ref, u_ref, vt_ref, e1_ref, t2_ref, e2_ref, s2_ref, o_ref,
                       h0_ref, h1_ref, w0_ref, w1_ref, *, n_blocks):
    s = pl.program_id(1)
    n_chunks, eb, _ = w0_ref.shape
    per_block = eb // N_KEYS

    @pl.when(s == 0)
    def _():
        o_ref[...] = jnp.zeros_like(o_ref)
        for ref in (h0_ref, h1_ref, w0_ref, w1_ref):
            ref[...] = jnp.zeros_like(ref)

    def stage(h_new, h_old, w_new, w_old, second):
        blk = s - 1
        live = jnp.where((blk >= 0) & (blk < n_blocks), 1.0, 0.0)
        pair = jnp.clip(blk // 2, 0, n_blocks // 2 - 1)

        def hidden_piece(m0, c0, k0, nk):
            res = jnp.dot(u_ref[m0:m0 + eb // 2, k0:k0 + nk],
                          ft_ref[k0:k0 + nk, c0 * LANES:(c0 + 2) * LANES],
                          preferred_element_type=F32)
            if k0 == 0:
                h_new[c0, m0:m0 + eb // 2, :] = res[:, :LANES]
                h_new[c0 + 1, m0:m0 + eb // 2, :] = res[:, LANES:]
            else:
                h_new[c0, m0:m0 + eb // 2, :] += res[:, :LANES]
                h_new[c0 + 1, m0:m0 + eb // 2, :] += res[:, LANES:]

        def out_piece(r0, nr):
            w = jnp.concatenate([w_old[c] for c in range(n_chunks)], axis=1)
            o_ref[r0:r0 + nr, :] += jnp.dot(vt_ref[r0:r0 + nr, :], w, preferred_element_type=F32)

        def gate_tile(a, c):
            r = per_block * second + a
            rows = slice(a * N_KEYS, (a + 1) * N_KEYS)
            gate = jnp.zeros((N_KEYS, LANES), F32)
            for h in range(PEER_HEADS):
                t2 = t2_ref[h, c, pair, r:r + 1, :]
                e1 = e1_ref[h, c, pair, r:r + 1, :] * live
                gate += jnp.where(s2_ref[h, c] >= t2, e2_ref[h, c] * e1, 0.0)
            x = h_old[c, rows, :]
            gelu = 0.5 * x * (1.0 + lax.erf(x * (2.0 ** -0.5)))
            w_new[c, rows, :] = (gate * gelu).astype(BF16)

        d = o_ref.shape[0]
        nk = d // HIDDEN_K_PIECES
        nr = d // OUT_ROW_PIECES
        pieces = [functools.partial(hidden_piece, m0, c0, k0, nk) for k0 in range(0, d, nk)
                  for m0 in (0, eb // 2) for c0 in range(0, n_chunks, 2)]
        pieces += [functools.partial(out_piece, r0, nr) for r0 in range(0, d, nr)]
        tiles = [(a, c) for a in range(per_block) for c in range(n_chunks)]
        done = 0
        for q, piece in enumerate(pieces):
            piece()
            want = (q + 1) * len(tiles) // len(pieces)
            for a, c in tiles[done:want]:
                gate_tile(a, c)
            done = want

    @pl.when(s % 2 == 0)
    def _():
        stage(h0_ref, h1_ref, w1_ref, w0_ref, 1)

    @pl.when(s % 2 == 1)
    def _():
        stage(h1_ref, h0_ref, w0_ref, w1_ref, 0)


def _peer_dense(f_t, u_bf, v_t, e1, t2, e2, scores, *, t, eb):
    d, n = f_t.shape
    ne = u_bf.shape[0]
    assert ne % (2 * eb) == 0 and 2 * eb == 8 * N_KEYS
    assert t % (2 * LANES) == 0
    n_blocks = ne // eb
    cpt = t // LANES
    once = dict(pipeline_mode=pl.Buffered(1))
    gspec = pl.BlockSpec((PEER_HEADS, cpt, N_KEYS, LANES), lambda i, s: (0, i, 0, 0), **once)
    pspec = pl.BlockSpec((PEER_HEADS, cpt, N_KEYS // 8, 8, LANES), lambda i, s: (0, i, 0, 0, 0),
                         **once)
    s2spec = pl.BlockSpec((PEER_HEADS, None, cpt, N_KEYS, LANES), lambda i, s: (0, 1, i, 0, 0),
                          **once)
    e1 = e1.reshape(PEER_HEADS, n // LANES, N_KEYS // 8, 8, LANES)
    t2 = t2.reshape(PEER_HEADS, n // LANES, N_KEYS // 8, 8, LANES)
    scores = scores.reshape(PEER_HEADS, 2, n // LANES, N_KEYS, LANES)
    kern = functools.partial(_peer_dense_kernel, n_blocks=n_blocks)
    return pl.pallas_call(
        kern,
        grid=(n // t, n_blocks + 2),
        in_specs=[pl.BlockSpec((d, t), lambda i, s: (0, i), **once),
                  pl.BlockSpec((eb, d), lambda i, s: (jnp.minimum(s, n_blocks - 1), 0)),
                  pl.BlockSpec((None, d, eb), lambda i, s: (jnp.clip(s - 2, 0, n_blocks - 1), 0, 0)),
                  pspec, pspec, gspec, s2spec],
        out_specs=pl.BlockSpec((d, t), lambda i, s: (0, i)),
        out_shape=jax.ShapeDtypeStruct((d, n), F32),
        scratch_shapes=[pltpu.VMEM((cpt, eb, LANES), F32), pltpu.VMEM((cpt, eb, LANES), F32),
                        pltpu.VMEM((cpt, eb, LANES), BF16), pltpu.VMEM((cpt, eb, LANES), BF16)],
        compiler_params=_cparams(("arbitrary", "arbitrary")),
        name="peer_dense",
    )(f_t, u_bf, v_t, e1, t2, e2, scores)


def _ple_kernel(h1_ref, peer_ref, p_ref, pn_ref, wg_ref, wp_ref, fin_ref, o_ref, *, last_layer):
    h2 = h1_ref[...] + peer_ref[...].T
    gate = jax.nn.sigmoid(jnp.dot(_rms(h2, pn_ref[...]).astype(BF16), wg_ref[...],
                                  preferred_element_type=F32))
    proj = jnp.dot(p_ref[...].astype(BF16), wp_ref[...], preferred_element_type=F32)
    h3 = h2 + proj * gate
    o_ref[...] = _rms(h3, fin_ref[...]) if last_layer else h3


def _ple_final(h1, peer, p2, ple_norm, w_gate, w_proj, final_norm, *, tm, last_layer):
    n, d = h1.shape
    tok = lambda w: pl.BlockSpec((tm, w), lambda i: (i, 0))
    return pl.pallas_call(
        functools.partial(_ple_kernel, last_layer=last_layer),
        grid=(n // tm,),
        in_specs=[tok(d), pl.BlockSpec((d, tm), lambda i: (0, i)), tok(p2.shape[1]),
                  _resident((1, d)), _resident(w_gate.shape),
                  _resident(w_proj.shape), _resident((1, d))],
        out_specs=tok(d),
        out_shape=jax.ShapeDtypeStruct((n, d), F32),
        compiler_params=_cparams(("arbitrary",)),
        name="ple_final",
    )(h1, peer, p2, ple_norm, w_gate, w_proj, final_norm)


def _swap_halves(w):
    half = w.shape[-1] // 2
    return jnp.concatenate([w[..., half:], w[..., :half]], axis=-1)


def _tile(n, pref):
    t = min(n, pref)
    assert n % t == 0
    return t


def kernel(x, p, positions, attn_norm, w_in, conv_w, q_norm, w_uq, kv_norm, w_ukv, conv_out_norm,
           attn_out_norm, w_out, ffn_norm, w_pq, sub_keys, u_tab, v_tab, ple_norm, w_ple_gate,
           w_ple_proj, final_norm):
    batch, seq, d = x.shape
    n = batch * seq
    depth = w_in.shape[0]
    row = lambda g: g.reshape(1, -1)

    inv_freq = ROPE_THETA ** (-(jnp.arange(0, QK_ROPE, 2, dtype=F32) / QK_ROPE))
    zeros = jnp.zeros((LANES - QK_ROPE,), F32)
    freq_row = row(jnp.concatenate([inv_freq, inv_freq, zeros]))
    sign_row = row(jnp.concatenate([-jnp.ones_like(inv_freq), jnp.ones_like(inv_freq), zeros]))
    pos2 = positions.reshape(n, 1)

    h = x.reshape(n, d)
    for i in range(depth):
        kr_cols = w_in[i][:, -QK_ROPE:]
        w_in_ext = jnp.concatenate([w_in[i], _swap_halves(kr_cols)], axis=-1).astype(BF16)
        uq = w_uq[i].reshape(-1, N_HEADS, QK_NOPE + QK_ROPE)
        uq_rope = uq[..., QK_NOPE:]
        w_uq_ext = jnp.concatenate([uq, _swap_halves(uq_rope)], axis=-1)
        w_uq_ext = w_uq_ext.reshape(-1, N_HEADS * HEAD_PAD).astype(BF16)
        ukv = w_ukv[i].reshape(-1, N_HEADS, QK_NOPE + V_HEAD)
        w_ukv_p = jnp.concatenate([ukv[..., :QK_NOPE].reshape(-1, N_HEADS * QK_NOPE),
                                   ukv[..., QK_NOPE:].reshape(-1, N_HEADS * V_HEAD)],
                                  axis=-1).astype(BF16)
        keys = sub_keys[i].reshape(-1, N_KEYS, sub_keys.shape[-1]).astype(BF16)

        conv_n, q_pad, k_pad, v = _in_proj(
            h, pos2, row(attn_norm[i]), w_in_ext, conv_w[i], row(q_norm[i]), w_uq_ext,
            row(kv_norm[i]), w_ukv_p, row(conv_out_norm[i]), freq_row, sign_row,
            seq=seq, tm=_tile(seq, 512))
        attn_n = _attention(q_pad, k_pad, v, row(attn_out_norm[i]), batch=batch, seq=seq,
                            tq=_tile(seq, 2048), tk=_tile(seq, 512), rc=_tile(seq, 512))
        h1, f_t, scores = _out_proj(conv_n, attn_n, h, w_out[i].astype(BF16), row(ffn_norm[i]),
                                    w_pq[i].T.astype(BF16), keys, tm=_tile(n, 512))
        e1, t2, e2 = _peer_gate(scores, tm=_tile(n, 512))
        eb = 512
        v_blocks = v_tab[i].reshape(-1, eb, d).transpose(0, 2, 1).astype(BF16)
        peer = _peer_dense(f_t, u_tab[i].astype(BF16), v_blocks, e1, t2, e2, scores,
                           t=_tile(n, 1024), eb=eb)
        h = _ple_final(h1, peer, p[i].reshape(n, -1), row(ple_norm[i]),
                       w_ple_gate[i].astype(BF16), w_ple_proj[i].astype(BF16),
                       row(final_norm), tm=_tile(n, 512), last_layer=i == depth - 1)
    return h.reshape(batch, seq, d)
```

```python
import functools

import jax
import jax.numpy as jnp
from jax import lax
from jax.experimental import pallas as pl
from jax.experimental.pallas import tpu as pltpu

EPS = 1e-6
ROPE_THETA = 10000.0
CONV_GROUP_DIM = 128
N_HEADS = 8
QK_NOPE = 128
QK_ROPE = 64
V_HEAD = 128
HEAD_PAD = 256
PEER_HEADS = 8
PEER_TOPK = 16
N_KEYS = 128
LANES = 128
HIDDEN_K_PIECES = 4
HIDDEN_ROW_PIECES = 1
OUT_ROW_PIECES = 16
NEG_BIG = -1e30

VMEM_LIMIT = 56 * 1024 * 1024

F32 = jnp.float32
BF16 = jnp.bfloat16


def _cparams(semantics):
    return pltpu.CompilerParams(dimension_semantics=semantics, vmem_limit_bytes=VMEM_LIMIT)


def _resident(shape):
    nd = len(shape)
    return pl.BlockSpec(shape, lambda *_: (0,) * nd, pipeline_mode=pl.Buffered(1))


def _rms(xf, g):
    return xf * lax.rsqrt(jnp.mean(xf * xf, axis=-1, keepdims=True) + EPS) * g


def _in_proj_kernel(x_ref, pos_ref, an_ref, win_ref, cw_ref, qn_ref, wuq_ref, kvn_ref, wukv_ref,
                    con_ref, freq_ref, sign_ref,
                    conv_ref, q_ref, kt_ref, v_ref, carry_ref, *, tiles_per_seq, conv_ch, q_lora,
                    kv_lora):
    i = pl.program_id(0)
    tm = x_ref.shape[0]

    @pl.when(i % tiles_per_seq == 0)
    def _():
        carry_ref[...] = jnp.zeros_like(carry_ref)

    a = _rms(x_ref[...], an_ref[...]).astype(BF16)

    xin = jnp.dot(a, win_ref[:, 0:conv_ch], preferred_element_type=F32)
    c_g = jnp.dot(a, win_ref[:, 2 * conv_ch:3 * conv_ch], preferred_element_type=F32)
    u = c_g * xin
    prev = carry_ref[...]
    row = lax.broadcasted_iota(jnp.int32, u.shape, 0)
    u1 = jnp.where(row == 0, prev[7:8], pltpu.roll(u, 1, 0))
    u2 = jnp.where(row == 0, prev[6:7], jnp.where(row == 1, prev[7:8], pltpu.roll(u, 2, 0)))
    carry_ref[...] = u[tm - 8:tm]
    y = cw_ref[2:3] * u + cw_ref[1:2] * u1 + cw_ref[0:1] * u2
    b_g = jnp.dot(a, win_ref[:, conv_ch:2 * conv_ch], preferred_element_type=F32)
    conv_o = b_g * y
    for g in range(conv_ch // CONV_GROUP_DIM):
        sl = slice(g * CONV_GROUP_DIM, (g + 1) * CONV_GROUP_DIM)
        conv_ref[:, sl] = _rms(conv_o[:, sl], con_ref[:, sl]).astype(BF16)

    m0 = 3 * conv_ch
    lat = jnp.dot(a, win_ref[:, m0:], preferred_element_type=F32)
    ang = pos_ref[...].astype(F32) * freq_ref[...]
    cos_t = jnp.cos(ang) * jnp.abs(sign_ref[...])
    sin_t = jnp.sin(ang) * sign_ref[...]

    def rope(blk):
        return blk * cos_t + pltpu.roll(blk, QK_ROPE, 1) * sin_t

    cq = _rms(lat[:, 0:q_lora], qn_ref[...]).astype(BF16)
    qf = jnp.dot(cq, wuq_ref[...], preferred_element_type=F32)
    ckv = _rms(lat[:, q_lora:q_lora + kv_lora], kvn_ref[...]).astype(BF16)
    kvf = jnp.dot(ckv, wukv_ref[...], preferred_element_type=F32)
    kr_t = rope(lat[:, q_lora + kv_lora:]).T.astype(BF16)
    for h in range(N_HEADS):
        o = h * HEAD_PAD
        q_ref[:, o:o + QK_NOPE] = qf[:, o:o + QK_NOPE].astype(BF16)
        q_ref[:, o + QK_NOPE:o + HEAD_PAD] = rope(qf[:, o + QK_NOPE:o + HEAD_PAD]).astype(BF16)
        kt_ref[o:o + QK_NOPE, :] = kvf[:, h * QK_NOPE:(h + 1) * QK_NOPE].T.astype(BF16)
        kt_ref[o + QK_NOPE:o + HEAD_PAD, :] = kr_t
    v_ref[...] = kvf[:, N_HEADS * QK_NOPE:].astype(BF16)


def _in_proj(x2, pos2, attn_norm, w_in_ext, conv_w, q_norm, w_uq_ext, kv_norm, w_ukv_p,
             conv_out_norm, freq_row, sign_row, *, seq, tm):
    n, d = x2.shape
    conv_ch = conv_w.shape[-1]
    q_lora, kv_lora = q_norm.shape[-1], kv_norm.shape[-1]
    assert seq % tm == 0 and n % seq == 0
    tok = lambda w: pl.BlockSpec((tm, w), lambda i: (i, 0))
    kern = functools.partial(_in_proj_kernel, tiles_per_seq=seq // tm, conv_ch=conv_ch,
                             q_lora=q_lora, kv_lora=kv_lora)
    return pl.pallas_call(
        kern,
        grid=(n // tm,),
        in_specs=[tok(d), tok(1), _resident((1, d)), _resident(w_in_ext.shape),
                  _resident(conv_w.shape), _resident((1, q_lora)), _resident(w_uq_ext.shape),
                  _resident((1, kv_lora)), _resident(w_ukv_p.shape), _resident((1, conv_ch)),
                  _resident((1, LANES)), _resident((1, LANES))],
        out_specs=[tok(conv_ch), tok(N_HEADS * HEAD_PAD),
                   pl.BlockSpec((N_HEADS * HEAD_PAD, tm), lambda i: (0, i)),
                   tok(N_HEADS * V_HEAD)],
        out_shape=[jax.ShapeDtypeStruct((n, conv_ch), BF16),
                   jax.ShapeDtypeStruct((n, N_HEADS * HEAD_PAD), BF16),
                   jax.ShapeDtypeStruct((N_HEADS * HEAD_PAD, n), BF16),
                   jax.ShapeDtypeStruct((n, N_HEADS * V_HEAD), BF16)],
        scratch_shapes=[pltpu.VMEM((8, conv_ch), F32)],
        compiler_params=_cparams(("arbitrary",)),
        name="in_proj",
    )(x2, pos2, attn_norm, w_in_ext, conv_w, q_norm, w_uq_ext, kv_norm, w_ukv_p, conv_out_norm,
      freq_row, sign_row)


def _attn_kernel(q_ref, kt_ref, v_ref, g_ref, o_ref, m_ref, l_ref, acc_ref, *, tk, rc, coef):
    qi = pl.program_id(2)
    tq = q_ref.shape[0]
    n_diag = tq // tk
    m_ref[...] = jnp.full_like(m_ref, NEG_BIG)
    l_ref[...] = jnp.zeros_like(l_ref)
    acc_ref[...] = jnp.zeros_like(acc_ref)

    def chunk_step(kb, r0, mask_offset):
        rows = slice(r0, r0 + rc)
        start = pl.multiple_of(kb * tk, tk)
        kt = kt_ref[:, pl.ds(start, tk)]
        v = v_ref[pl.ds(start, tk), :]
        s = jnp.dot(q_ref[rows, :], kt, preferred_element_type=F32)
        slabs = [s[:, j * LANES:(j + 1) * LANES] for j in range(tk // LANES)]
        if mask_offset is not None:
            r = lax.broadcasted_iota(jnp.int32, (rc, LANES), 0) + r0
            c = lax.broadcasted_iota(jnp.int32, (rc, LANES), 1)
            slabs = [jnp.where(c + (mask_offset + j * LANES) <= r, sl, NEG_BIG)
                     for j, sl in enumerate(slabs)]
        m_old = m_ref[rows, :]
        m_new = jnp.maximum(m_old, jnp.max(functools.reduce(jnp.maximum, slabs), axis=-1,
                                           keepdims=True))
        alpha = jnp.exp2((m_old - m_new) * coef)
        ps = [jnp.exp2((sl - m_new) * coef) for sl in slabs]
        l_ref[rows, :] = alpha * l_ref[rows, :] + jnp.sum(functools.reduce(jnp.add, ps), axis=-1,
                                                          keepdims=True)
        p = jnp.concatenate([x.astype(BF16) for x in ps], axis=1)
        acc_ref[rows, :] = alpha * acc_ref[rows, :] + jnp.dot(p, v, preferred_element_type=F32)
        m_ref[rows, :] = m_new

    n_full = qi * n_diag

    def body(kb, carry):
        for r0 in range(0, tq, rc):
            chunk_step(kb, r0, None)
        return carry

    lax.fori_loop(0, n_full, body, 0)
    for j in range(n_diag):
        for r0 in range(0, tq, rc):
            if r0 + rc - 1 < j * tk:
                continue
            needs_mask = r0 < j * tk + tk - 1
            chunk_step(n_full + j, r0, j * tk if needs_mask else None)

    o = acc_ref[...] / l_ref[...]
    o_ref[...] = _rms(o, g_ref[...]).astype(BF16)


def _attention(q_pad, kt_pad, v, attn_out_norm, *, batch, seq, tq, tk, rc):
    n = q_pad.shape[0]
    assert seq % tq == 0 and tq % tk == 0 and tq % rc == 0 and tk % LANES == 0
    nq = seq // tq
    coef = float(QK_NOPE + QK_ROPE) ** -0.5 * 1.4426950408889634
    kern = functools.partial(_attn_kernel, tk=tk, rc=rc, coef=coef)
    return pl.pallas_call(
        kern,
        grid=(batch, N_HEADS, nq),
        in_specs=[pl.BlockSpec((tq, HEAD_PAD), lambda b, h, i: (b * nq + i, h)),
                  pl.BlockSpec((HEAD_PAD, seq), lambda b, h, i: (h, b)),
                  pl.BlockSpec((seq, V_HEAD), lambda b, h, i: (b, h)),
                  pl.BlockSpec((1, V_HEAD), lambda b, h, i: (0, h))],
        out_specs=pl.BlockSpec((tq, V_HEAD), lambda b, h, i: (b * nq + i, h)),
        out_shape=jax.ShapeDtypeStruct((n, N_HEADS * V_HEAD), BF16),
        scratch_shapes=[pltpu.VMEM((tq, LANES), F32), pltpu.VMEM((tq, LANES), F32),
                        pltpu.VMEM((tq, V_HEAD), F32)],
        compiler_params=_cparams(("arbitrary", "arbitrary", "arbitrary")),
        name="attention",
    )(q_pad, kt_pad, v, attn_out_norm)


def _out_proj_kernel(conv_ref, attn_ref, x_ref, wo_ref, fn_ref, wpqt_ref, keys_ref,
                     h1_ref, ft_ref, sc_ref):
    conv_ch = conv_ref.shape[1]
    mix = jnp.dot(conv_ref[...], wo_ref[0:conv_ch, :], preferred_element_type=F32)
    mix += jnp.dot(attn_ref[...], wo_ref[conv_ch:, :], preferred_element_type=F32)
    h1 = x_ref[...] + mix
    h1_ref[...] = h1
    ft = _rms(h1, fn_ref[...]).T.astype(BF16)
    ft_ref[...] = ft
    qpt = jnp.dot(wpqt_ref[...], ft, preferred_element_type=F32).astype(BF16)
    for hp in range(keys_ref.shape[0]):
        sc = jnp.dot(keys_ref[hp], qpt[hp * N_KEYS:(hp + 1) * N_KEYS, :],
                     preferred_element_type=F32)
        for c in range(sc_ref.shape[1]):
            sc_ref[hp, c] = sc[:, c * LANES:(c + 1) * LANES]


def _out_proj(conv_n, attn_n, x2, w_out, ffn_norm, w_pq_t, keys, *, tm):
    n, d = x2.shape
    nhp = keys.shape[0]
    tok = lambda w: pl.BlockSpec((tm, w), lambda i: (i, 0))
    return pl.pallas_call(
        _out_proj_kernel,
        grid=(n // tm,),
        in_specs=[tok(conv_n.shape[1]), tok(attn_n.shape[1]), tok(d), _resident(w_out.shape),
                  _resident((1, d)), _resident(w_pq_t.shape), _resident(keys.shape)],
        out_specs=[tok(d), pl.BlockSpec((d, tm), lambda i: (0, i)),
                   pl.BlockSpec((nhp, tm // LANES, N_KEYS, LANES), lambda i: (0, i, 0, 0))],
        out_shape=[jax.ShapeDtypeStruct((n, d), F32), jax.ShapeDtypeStruct((d, n), BF16),
                   jax.ShapeDtypeStruct((nhp, n // LANES, N_KEYS, LANES), F32)],
        compiler_params=_cparams(("arbitrary",)),
        name="out_proj",
    )(conv_n, attn_n, x2, w_out, ffn_norm, w_pq_t, keys)


def _top16(s):
    c = s.shape[1]
    rows = lax.broadcasted_iota(jnp.int32, (PEER_TOPK, c), 0)
    vals = jnp.zeros((PEER_TOPK, c), F32)
    x = s
    for k in range(PEER_TOPK):
        m = jnp.max(x, axis=0, keepdims=True)
        x = jnp.where(x == m, -jnp.inf, x)
        vals = jnp.where(rows == k, m, vals)
    return vals


def _peer_gate_kernel(sc_ref, e1_ref, t2_ref, e2_ref):
    n_chunks = sc_ref.shape[1]

    def one(h, c):
        s1 = sc_ref[2 * h, c]
        s2 = sc_ref[2 * h + 1, c]
        v1 = _top16(s1)
        v2 = _top16(s2)
        cand = jnp.concatenate([v1 + v2[0:1]] + [v1[0:8] + v2[b:b + 1] for b in range(1, 8)]
                               + [v1[0:1] + v2[8:16]], axis=0)
        best = v1[0:1] + v2[0:1]
        zsum = jnp.zeros_like(best)
        for _ in range(PEER_TOPK):
            thr = jnp.max(cand, axis=0, keepdims=True)
            cand = jnp.where(cand == thr, -jnp.inf, cand)
            zsum += jnp.exp(thr - best)
        tsel = jnp.full_like(v1, jnp.inf)
        for b in range(PEER_TOPK):
            tsel = jnp.where(v1 + v2[b:b + 1] >= thr, v2[b:b + 1], tsel)
        t2 = jnp.full_like(s1, jnp.inf)
        for a in range(PEER_TOPK):
            t2 = jnp.where(s1 == v1[a:a + 1], tsel[a:a + 1], t2)
        e1_ref[h, c] = jnp.exp(s1 - v1[0:1])
        t2_ref[h, c] = t2
        e2_ref[h, c] = jnp.exp(s2 - v2[0:1]) / zsum

    def body(t, carry):
        one(t // n_chunks, t % n_chunks)
        return carry

    lax.fori_loop(0, PEER_HEADS * n_chunks, body, 0)


def _peer_gate(scores, *, tm):
    nhp, n_chunks, nk, _ = scores.shape
    cpt = tm // LANES
    out = [jax.ShapeDtypeStruct((nhp // 2, n_chunks, nk, LANES), F32)] * 3
    spec = pl.BlockSpec((nhp // 2, cpt, nk, LANES), lambda i: (0, i, 0, 0))
    return pl.pallas_call(
        _peer_gate_kernel,
        grid=(n_chunks // cpt,),
        in_specs=[pl.BlockSpec((nhp, cpt, nk, LANES), lambda i: (0, i, 0, 0))],
        out_specs=[spec, spec, spec],
        out_shape=out,
        compiler_params=_cparams(("arbitrary",)),
        name="peer_gate",
    )(scores)


def _peer_dense_kernel(ft_ref, u_ref, vt_ref, e1_ref, t2_ref, e2_ref, s2_ref, o_ref,
                       h0_ref, h1_ref, w0_ref, w1_ref, *, n_blocks):
    s = pl.program_id(1)
    n_chunks, eb, _ = w0_ref.shape
    per_block = eb // N_KEYS

    @pl.when(s == 0)
    def _():
        o_ref[...] = jnp.zeros_like(o_ref)
        for ref in (h0_ref, h1_ref, w0_ref, w1_ref):
            ref[...] = jnp.zeros_like(ref)

    def stage(h_new, h_old, w_new, w_old, second):
        blk = s - 1
        live = jnp.where((blk >= 0) & (blk < n_blocks), 1.0, 0.0)
        pair = jnp.clip(blk // 2, 0, n_blocks // 2 - 1)
        hm = eb // HIDDEN_ROW_PIECES

        def hidden_piece(m0, c0, k0, nk):
            res = jnp.dot(u_ref[m0:m0 + hm, k0:k0 + nk],
                          ft_ref[k0:k0 + nk, c0 * LANES:(c0 + 2) * LANES],
                          preferred_element_type=F32)
            if k0 == 0:
                h_new[c0, m0:m0 + hm, :] = res[:, :LANES]
                h_new[c0 + 1, m0:m0 + hm, :] = res[:, LANES:]
            else:
                h_new[c0, m0:m0 + hm, :] += res[:, :LANES]
                h_new[c0 + 1, m0:m0 + hm, :] += res[:, LANES:]

        def out_piece(r0, nr):
            w = jnp.concatenate([w_old[c] for c in range(n_chunks)], axis=1)
            o_ref[r0:r0 + nr, :] += jnp.dot(vt_ref[r0:r0 + nr, :], w, preferred_element_type=F32)

        def gate_tile(a, c):
            r = per_block * second + a
            rows = slice(a * N_KEYS, (a + 1) * N_KEYS)
            gate = jnp.zeros((N_KEYS, LANES), F32)
            for h in range(PEER_HEADS):
                t2 = t2_ref[h, c, pair, r:r + 1, :]
                e1 = e1_ref[h, c, pair, r:r + 1, :] * live
                gate += jnp.where(s2_ref[h, c] >= t2, e2_ref[h, c] * e1, 0.0)
            x = h_old[c, rows, :]
            gelu = 0.5 * x * (1.0 + lax.erf(x * (2.0 ** -0.5)))
            w_new[c, rows, :] = (gate * gelu).astype(BF16)

        d = o_ref.shape[0]
        nk = d // HIDDEN_K_PIECES
        nr = d // OUT_ROW_PIECES
        pieces = [functools.partial(hidden_piece, m0, c0, k0, nk) for k0 in range(0, d, nk)
                  for m0 in range(0, eb, hm) for c0 in range(0, n_chunks, 2)]
        pieces += [functools.partial(out_piece, r0, nr) for r0 in range(0, d, nr)]
        tiles = [(a, c) for a in range(per_block) for c in range(n_chunks)]
        done = 0
        for q, piece in enumerate(pieces):
            piece()
            want = (q + 1) * len(tiles) // len(pieces)
            for a, c in tiles[done:want]:
                gate_tile(a, c)
            done = want

    @pl.when(s % 2 == 0)
    def _():
        stage(h0_ref, h1_ref, w1_ref, w0_ref, 1)

    @pl.when(s % 2 == 1)
    def _():
        stage(h1_ref, h0_ref, w0_ref, w1_ref, 0)


def _peer_dense(f_t, u_bf, v_t, e1, t2, e2, scores, *, t, eb):
    d, n = f_t.shape
    ne = u_bf.shape[0]
    assert ne % (2 * eb) == 0 and 2 * eb == 8 * N_KEYS
    assert t % (2 * LANES) == 0
    n_blocks = ne // eb
    cpt = t // LANES
    once = dict(pipeline_mode=pl.Buffered(1))
    gspec = pl.BlockSpec((PEER_HEADS, cpt, N_KEYS, LANES), lambda i, s: (0, i, 0, 0), **once)
    pspec = pl.BlockSpec((PEER_HEADS, cpt, N_KEYS // 8, 8, LANES), lambda i, s: (0, i, 0, 0, 0),
                         **once)
    s2spec = pl.BlockSpec((PEER_HEADS, None, cpt, N_KEYS, LANES), lambda i, s: (0, 1, i, 0, 0),
                          **once)
    e1 = e1.reshape(PEER_HEADS, n // LANES, N_KEYS // 8, 8, LANES)
    t2 = t2.reshape(PEER_HEADS, n // LANES, N_KEYS // 8, 8, LANES)
    scores = scores.reshape(PEER_HEADS, 2, n // LANES, N_KEYS, LANES)
    kern = functools.partial(_peer_dense_kernel, n_blocks=n_blocks)
    return pl.pallas_call(
        kern,
        grid=(n // t, n_blocks + 2),
        in_specs=[pl.BlockSpec((d, t), lambda i, s: (0, i), **once),
                  pl.BlockSpec((eb, d), lambda i, s: (jnp.minimum(s, n_blocks - 1), 0)),
                  pl.BlockSpec((None, d, eb), lambda i, s: (jnp.clip(s - 2, 0, n_blocks - 1), 0, 0)),
                  pspec, pspec, gspec, s2spec],
        out_specs=pl.BlockSpec((d, t), lambda i, s: (0, i)),
        out_shape=jax.ShapeDtypeStruct((d, n), F32),
        scratch_shapes=[pltpu.VMEM((cpt, eb, LANES), F32), pltpu.VMEM((cpt, eb, LANES), F32),
                        pltpu.VMEM((cpt, eb, LANES), BF16), pltpu.VMEM((cpt, eb, LANES), BF16)],
        compiler_params=_cparams(("arbitrary", "arbitrary")),
        name="peer_dense",
    )(f_t, u_bf, v_t, e1, t2, e2, scores)


def _ple_kernel(h1_ref, peer_ref, p_ref, pn_ref, wg_ref, wp_ref, fin_ref, o_ref, *, last_layer):
    h2 = h1_ref[...] + peer_ref[...].T
    gate = jax.nn.sigmoid(jnp.dot(_rms(h2, pn_ref[...]).astype(BF16), wg_ref[...],
                                  preferred_element_type=F32))
    proj = jnp.dot(p_ref[...].astype(BF16), wp_ref[...], preferred_element_type=F32)
    h3 = h2 + proj * gate
    o_ref[...] = _rms(h3, fin_ref[...]) if last_layer else h3


def _ple_final(h1, peer, p2, ple_norm, w_gate, w_proj, final_norm, *, tm, last_layer):
    n, d = h1.shape
    tok = lambda w: pl.BlockSpec((tm, w), lambda i: (i, 0))
    return pl.pallas_call(
        functools.partial(_ple_kernel, last_layer=last_layer),
        grid=(n // tm,),
        in_specs=[tok(d), pl.BlockSpec((d, tm), lambda i: (0, i)), tok(p2.shape[1]),
                  _resident((1, d)), _resident(w_gate.shape),
                  _resident(w_proj.shape), _resident((1, d))],
        out_specs=tok(d),
        out_shape=jax.ShapeDtypeStruct((n, d), F32),
        compiler_params=_cparams(("arbitrary",)),
        name="ple_final",
    )(h1, peer, p2, ple_norm, w_gate, w_proj, final_norm)


def _swap_halves(w):
    half = w.shape[-1] // 2
    return jnp.concatenate([w[..., half:], w[..., :half]], axis=-1)


def _tile(n, pref):
    t = min(n, pref)
    assert n % t == 0
    return t


def kernel(x, p, positions, attn_norm, w_in, conv_w, q_norm, w_uq, kv_norm, w_ukv, conv_out_norm,
           attn_out_norm, w_out, ffn_norm, w_pq, sub_keys, u_tab, v_tab, ple_norm, w_ple_gate,
           w_ple_proj, final_norm):
    batch, seq, d = x.shape
    n = batch * seq
    depth = w_in.shape[0]
    row = lambda g: g.reshape(1, -1)

    inv_freq = ROPE_THETA ** (-(jnp.arange(0, QK_ROPE, 2, dtype=F32) / QK_ROPE))
    zeros = jnp.zeros((LANES - QK_ROPE,), F32)
    freq_row = row(jnp.concatenate([inv_freq, inv_freq, zeros]))
    sign_row = row(jnp.concatenate([-jnp.ones_like(inv_freq), jnp.ones_like(inv_freq), zeros]))
    pos2 = positions.reshape(n, 1)

    h = x.reshape(n, d)
    for i in range(depth):
        kr_cols = w_in[i][:, -QK_ROPE:]
        w_in_ext = jnp.concatenate([w_in[i], _swap_halves(kr_cols)], axis=-1).astype(BF16)
        uq = w_uq[i].reshape(-1, N_HEADS, QK_NOPE + QK_ROPE)
        uq_rope = uq[..., QK_NOPE:]
        w_uq_ext = jnp.concatenate([uq, _swap_halves(uq_rope)], axis=-1)
        w_uq_ext = w_uq_ext.reshape(-1, N_HEADS * HEAD_PAD).astype(BF16)
        ukv = w_ukv[i].reshape(-1, N_HEADS, QK_NOPE + V_HEAD)
        w_ukv_p = jnp.concatenate([ukv[..., :QK_NOPE].reshape(-1, N_HEADS * QK_NOPE),
                                   ukv[..., QK_NOPE:].reshape(-1, N_HEADS * V_HEAD)],
                                  axis=-1).astype(BF16)
        keys = sub_keys[i].reshape(-1, N_KEYS, sub_keys.shape[-1]).astype(BF16)

        conv_n, q_pad, k_pad, v = _in_proj(
            h, pos2, row(attn_norm[i]), w_in_ext, conv_w[i], row(q_norm[i]), w_uq_ext,
            row(kv_norm[i]), w_ukv_p, row(conv_out_norm[i]), freq_row, sign_row,
            seq=seq, tm=_tile(seq, 512))
        attn_n = _attention(q_pad, k_pad, v, row(attn_out_norm[i]), batch=batch, seq=seq,
                            tq=_tile(seq, 2048), tk=_tile(seq, 1024), rc=_tile(seq, 1024))
        h1, f_t, scores = _out_proj(conv_n, attn_n, h, w_out[i].astype(BF16), row(ffn_norm[i]),
                                    w_pq[i].T.astype(BF16), keys, tm=_tile(n, 512))
        e1, t2, e2 = _peer_gate(scores, tm=_tile(n, 512))
        eb = 512
        v_blocks = v_tab[i].reshape(-1, eb, d).transpose(0, 2, 1).astype(BF16)
        peer = _peer_dense(f_t, u_tab[i].astype(BF16), v_blocks, e1, t2, e2, scores,
                           t=_tile(n, 1024), eb=eb)
        h = _ple_final(h1, peer, p[i].reshape(n, -1), row(ple_norm[i]),
                       w_ple_gate[i].astype(BF16), w_ple_proj[i].astype(BF16),
                       row(final_norm), tm=_tile(n, 512), last_layer=i == depth - 1)
    return h.reshape(batch, seq, d)
```

```python
import functools

import jax
import jax.numpy as jnp
from jax import lax
from jax.experimental import pallas as pl
from jax.experimental.pallas import tpu as pltpu

EPS = 1e-6
ROPE_THETA = 10000.0
CONV_GROUP_DIM = 128
N_HEADS = 8
QK_NOPE = 128
QK_ROPE = 64
V_HEAD = 128
HEAD_PAD = 256
PEER_HEADS = 8
PEER_TOPK = 16
N_KEYS = 128
LANES = 128
HIDDEN_K_PIECES = 4
HIDDEN_ROW_PIECES = 1
OUT_ROW_PIECES = 16
NEG_BIG = -1e30

VMEM_LIMIT = 56 * 1024 * 1024

F32 = jnp.float32
BF16 = jnp.bfloat16


def _cparams(semantics):
    return pltpu.CompilerParams(dimension_semantics=semantics, vmem_limit_bytes=VMEM_LIMIT)


def _resident(shape):
    nd = len(shape)
    return pl.BlockSpec(shape, lambda *_: (0,) * nd, pipeline_mode=pl.Buffered(1))


def _rms(xf, g):
    return xf * lax.rsqrt(jnp.mean(xf * xf, axis=-1, keepdims=True) + EPS) * g


def _in_proj_kernel(x_ref, pos_ref, an_ref, win_ref, cw_ref, qn_ref, wuq_ref, kvn_ref, wukv_ref,
                    con_ref, freq_ref, sign_ref,
                    conv_ref, q_ref, kt_ref, v_ref, carry_ref, *, tiles_per_seq, conv_ch, q_lora,
                    kv_lora):
    i = pl.program_id(0)
    tm = x_ref.shape[0]

    @pl.when(i % tiles_per_seq == 0)
    def _():
        carry_ref[...] = jnp.zeros_like(carry_ref)

    a = _rms(x_ref[...], an_ref[...]).astype(BF16)

    xin = jnp.dot(a, win_ref[:, 0:conv_ch], preferred_element_type=F32)
    c_g = jnp.dot(a, win_ref[:, 2 * conv_ch:3 * conv_ch], preferred_element_type=F32)
    u = c_g * xin
    prev = carry_ref[...]
    row = lax.broadcasted_iota(jnp.int32, u.shape, 0)
    u1 = jnp.where(row == 0, prev[7:8], pltpu.roll(u, 1, 0))
    u2 = jnp.where(row == 0, prev[6:7], jnp.where(row == 1, prev[7:8], pltpu.roll(u, 2, 0)))
    carry_ref[...] = u[tm - 8:tm]
    y = cw_ref[2:3] * u + cw_ref[1:2] * u1 + cw_ref[0:1] * u2
    b_g = jnp.dot(a, win_ref[:, conv_ch:2 * conv_ch], preferred_element_type=F32)
    conv_o = b_g * y
    for g in range(conv_ch // CONV_GROUP_DIM):
        sl = slice(g * CONV_GROUP_DIM, (g + 1) * CONV_GROUP_DIM)
        conv_ref[:, sl] = _rms(conv_o[:, sl], con_ref[:, sl]).astype(BF16)

    m0 = 3 * conv_ch
    lat = jnp.dot(a, win_ref[:, m0:], preferred_element_type=F32)
    ang = pos_ref[...].astype(F32) * freq_ref[...]
    cos_t = jnp.cos(ang) * jnp.abs(sign_ref[...])
    sin_t = jnp.sin(ang) * sign_ref[...]

    def rope(blk):
        return blk * cos_t + pltpu.roll(blk, QK_ROPE, 1) * sin_t

    cq = _rms(lat[:, 0:q_lora], qn_ref[...]).astype(BF16)
    qf = jnp.dot(cq, wuq_ref[...], preferred_element_type=F32)
    ckv = _rms(lat[:, q_lora:q_lora + kv_lora], kvn_ref[...]).astype(BF16)
    kvf = jnp.dot(ckv, wukv_ref[...], preferred_element_type=F32)
    kr_t = rope(lat[:, q_lora + kv_lora:]).T.astype(BF16)
    for h in range(N_HEADS):
        o = h * HEAD_PAD
        q_ref[:, o:o + QK_NOPE] = qf[:, o:o + QK_NOPE].astype(BF16)
        q_ref[:, o + QK_NOPE:o + HEAD_PAD] = rope(qf[:, o + QK_NOPE:o + HEAD_PAD]).astype(BF16)
        kt_ref[o:o + QK_NOPE, :] = kvf[:, h * QK_NOPE:(h + 1) * QK_NOPE].T.astype(BF16)
        kt_ref[o + QK_NOPE:o + HEAD_PAD, :] = kr_t
    v_ref[...] = kvf[:, N_HEADS * QK_NOPE:].astype(BF16)


def _in_proj(x2, pos2, attn_norm, w_in_ext, conv_w, q_norm, w_uq_ext, kv_norm, w_ukv_p,
             conv_out_norm, freq_row, sign_row, *, seq, tm):
    n, d = x2.shape
    conv_ch = conv_w.shape[-1]
    q_lora, kv_lora = q_norm.shape[-1], kv_norm.shape[-1]
    assert seq % tm == 0 and n % seq == 0
    tok = lambda w: pl.BlockSpec((tm, w), lambda i: (i, 0))
    kern = functools.partial(_in_proj_kernel, tiles_per_seq=seq // tm, conv_ch=conv_ch,
                             q_lora=q_lora, kv_lora=kv_lora)
    return pl.pallas_call(
        kern,
        grid=(n // tm,),
        in_specs=[tok(d), tok(1), _resident((1, d)), _resident(w_in_ext.shape),
                  _resident(conv_w.shape), _resident((1, q_lora)), _resident(w_uq_ext.shape),
                  _resident((1, kv_lora)), _resident(w_ukv_p.shape), _resident((1, conv_ch)),
                  _resident((1, LANES)), _resident((1, LANES))],
        out_specs=[tok(conv_ch), tok(N_HEADS * HEAD_PAD),
                   pl.BlockSpec((N_HEADS * HEAD_PAD, tm), lambda i: (0, i)),
                   tok(N_HEADS * V_HEAD)],
        out_shape=[jax.ShapeDtypeStruct((n, conv_ch), BF16),
                   jax.ShapeDtypeStruct((n, N_HEADS * HEAD_PAD), BF16),
                   jax.ShapeDtypeStruct((N_HEADS * HEAD_PAD, n), BF16),
                   jax.ShapeDtypeStruct((n, N_HEADS * V_HEAD), BF16)],
        scratch_shapes=[pltpu.VMEM((8, conv_ch), F32)],
        compiler_params=_cparams(("arbitrary",)),
        name="in_proj",
    )(x2, pos2, attn_norm, w_in_ext, conv_w, q_norm, w_uq_ext, kv_norm, w_ukv_p, conv_out_norm,
      freq_row, sign_row)


def _attn_kernel(q_ref, kt_ref, v_ref, g_ref, o_ref, m_ref, l_ref, acc_ref, *, tk, rc, coef):
    qi = pl.program_id(2)
    tq = q_ref.shape[0]
    n_diag = tq // tk
    m_ref[...] = jnp.full_like(m_ref, NEG_BIG)
    l_ref[...] = jnp.zeros_like(l_ref)
    acc_ref[...] = jnp.zeros_like(acc_ref)

    def chunk_step(kb, r0, mask_offset):
        rows = slice(r0, r0 + rc)
        start = pl.multiple_of(kb * tk, tk)
        kt = kt_ref[:, pl.ds(start, tk)]
        v = v_ref[pl.ds(start, tk), :]
        s = jnp.dot(q_ref[rows, :], kt, preferred_element_type=F32)
        slabs = [s[:, j * LANES:(j + 1) * LANES] for j in range(tk // LANES)]
        if mask_offset is not None:
            r = lax.broadcasted_iota(jnp.int32, (rc, LANES), 0) + r0
            c = lax.broadcasted_iota(jnp.int32, (rc, LANES), 1)
            slabs = [jnp.where(c + (mask_offset + j * LANES) <= r, sl, NEG_BIG)
                     for j, sl in enumerate(slabs)]
        m_old = m_ref[rows, :]
        m_new = jnp.maximum(m_old, jnp.max(functools.reduce(jnp.maximum, slabs), axis=-1,
                                           keepdims=True))
        alpha = jnp.exp2((m_old - m_new) * coef)
        ps = [jnp.exp2((sl - m_new) * coef) for sl in slabs]
        l_ref[rows, :] = alpha * l_ref[rows, :] + jnp.sum(functools.reduce(jnp.add, ps), axis=-1,
                                                          keepdims=True)
        p = jnp.concatenate([x.astype(BF16) for x in ps], axis=1)
        acc_ref[rows, :] = alpha * acc_ref[rows, :] + jnp.dot(p, v, preferred_element_type=F32)
        m_ref[rows, :] = m_new

    n_full = qi * n_diag

    def body(kb, carry):
        for r0 in range(0, tq, rc):
            chunk_step(kb, r0, None)
        return carry

    lax.fori_loop(0, n_full, body, 0)
    for j in range(n_diag):
        for r0 in range(0, tq, rc):
            if r0 + rc - 1 < j * tk:
                continue
            needs_mask = r0 < j * tk + tk - 1
            chunk_step(n_full + j, r0, j * tk if needs_mask else None)

    o = acc_ref[...] / l_ref[...]
    o_ref[...] = _rms(o, g_ref[...]).astype(BF16)


def _attention(q_pad, kt_pad, v, attn_out_norm, *, batch, seq, tq, tk, rc):
    n = q_pad.shape[0]
    assert seq % tq == 0 and tq % tk == 0 and tq % rc == 0 and tk % LANES == 0
    nq = seq // tq
    coef = float(QK_NOPE + QK_ROPE) ** -0.5 * 1.4426950408889634
    kern = functools.partial(_attn_kernel, tk=tk, rc=rc, coef=coef)
    return pl.pallas_call(
        kern,
        grid=(batch, N_HEADS, nq),
        in_specs=[pl.BlockSpec((tq, HEAD_PAD), lambda b, h, i: (b * nq + i, h)),
                  pl.BlockSpec((HEAD_PAD, seq), lambda b, h, i: (h, b)),
                  pl.BlockSpec((seq, V_HEAD), lambda b, h, i: (b, h)),
                  pl.BlockSpec((1, V_HEAD), lambda b, h, i: (0, h))],
        out_specs=pl.BlockSpec((tq, V_HEAD), lambda b, h, i: (b * nq + i, h)),
        out_shape=jax.ShapeDtypeStruct((n, N_HEADS * V_HEAD), BF16),
        scratch_shapes=[pltpu.VMEM((tq, LANES), F32), pltpu.VMEM((tq, LANES), F32),
                        pltpu.VMEM((tq, V_HEAD), F32)],
        compiler_params=_cparams(("arbitrary", "arbitrary", "arbitrary")),
        name="attention",
    )(q_pad, kt_pad, v, attn_out_norm)


def _out_proj_kernel(conv_ref, attn_ref, x_ref, wo_ref, fn_ref, wpqt_ref, keys_ref,
                     h1_ref, ft_ref, sc_ref):
    conv_ch = conv_ref.shape[1]
    mix = jnp.dot(conv_ref[...], wo_ref[0:conv_ch, :], preferred_element_type=F32)
    mix += jnp.dot(attn_ref[...], wo_ref[conv_ch:, :], preferred_element_type=F32)
    h1 = x_ref[...] + mix
    h1_ref[...] = h1
    ft = _rms(h1, fn_ref[...]).T.astype(BF16)
    ft_ref[...] = ft
    qpt = jnp.dot(wpqt_ref[...], ft, preferred_element_type=F32).astype(BF16)
    for hp in range(keys_ref.shape[0]):
        sc = jnp.dot(keys_ref[hp], qpt[hp * N_KEYS:(hp + 1) * N_KEYS, :],
                     preferred_element_type=F32)
        for c in range(sc_ref.shape[1]):
            sc_ref[hp, c] = sc[:, c * LANES:(c + 1) * LANES]


def _out_proj(conv_n, attn_n, x2, w_out, ffn_norm, w_pq_t, keys, *, tm):
    n, d = x2.shape
    nhp = keys.shape[0]
    tok = lambda w: pl.BlockSpec((tm, w), lambda i: (i, 0))
    return pl.pallas_call(
        _out_proj_kernel,
        grid=(n // tm,),
        in_specs=[tok(conv_n.shape[1]), tok(attn_n.shape[1]), tok(d), _resident(w_out.shape),
                  _resident((1, d)), _resident(w_pq_t.shape), _resident(keys.shape)],
        out_specs=[tok(d), pl.BlockSpec((d, tm), lambda i: (0, i)),
                   pl.BlockSpec((nhp, tm // LANES, N_KEYS, LANES), lambda i: (0, i, 0, 0))],
        out_shape=[jax.ShapeDtypeStruct((n, d), F32), jax.ShapeDtypeStruct((d, n), BF16),
                   jax.ShapeDtypeStruct((nhp, n // LANES, N_KEYS, LANES), F32)],
        compiler_params=_cparams(("arbitrary",)),
        name="out_proj",
    )(conv_n, attn_n, x2, w_out, ffn_norm, w_pq_t, keys)


def _sort_network(n):
    pairs = []

    def merge(lo, m, r):
        step = 2 * r
        if step < m:
            merge(lo, m, step)
            merge(lo + r, m, step)
            pairs.extend((i, i + r) for i in range(lo + r, lo + m - r, step))
        else:
            pairs.append((lo, lo + r))

    def sort(lo, m):
        if m > 1:
            sort(lo, m // 2)
            sort(lo + m // 2, m // 2)
            merge(lo, m, 1)

    sort(0, n)
    return pairs


def _top16(s):
    k = PEER_TOPK
    c = s.shape[1]
    g = [s[8 * j:8 * j + 8, :] for j in range(s.shape[0] // 8)]
    assert len(g) == k

    def exchange(i, j):
        g[i], g[j] = jnp.maximum(g[i], g[j]), jnp.minimum(g[i], g[j])

    for i, j in _sort_network(k):
        exchange(i, j)
    for shift in (4, 2, 1):
        other = [pltpu.roll(x, shift, 0) for x in g]
        g = [jnp.maximum(g[j], other[k - 1 - j]) for j in range(k)]
        dist = k // 2
        while dist:
            for i in range(k):
                if not i & dist:
                    exchange(i, i + dist)
            dist //= 2
    rows = lax.broadcasted_iota(jnp.int32, (k, c), 0)
    vals = jnp.zeros((k, c), F32)
    for j in range(k):
        vals = jnp.where(rows == j, jnp.concatenate([g[j], g[j]], axis=0), vals)
    return vals


def _peer_gate_kernel(sc_ref, e1_ref, t2_ref, e2_ref):
    n_chunks = sc_ref.shape[1]

    def one(h, c):
        s1 = sc_ref[2 * h, c]
        s2 = sc_ref[2 * h + 1, c]
        v1 = _top16(s1)
        v2 = _top16(s2)
        cand = jnp.concatenate([v1 + v2[0:1]] + [v1[0:8] + v2[b:b + 1] for b in range(1, 8)]
                               + [v1[0:1] + v2[8:16]], axis=0)
        best = v1[0:1] + v2[0:1]
        zsum = jnp.zeros_like(best)
        for _ in range(PEER_TOPK):
            thr = jnp.max(cand, axis=0, keepdims=True)
            cand = jnp.where(cand == thr, -jnp.inf, cand)
            zsum += jnp.exp(thr - best)
        tsel = jnp.full_like(v1, jnp.inf)
        for b in range(PEER_TOPK):
            tsel = jnp.where(v1 + v2[b:b + 1] >= thr, v2[b:b + 1], tsel)
        t2 = jnp.full_like(s1, jnp.inf)
        for a in range(PEER_TOPK):
            t2 = jnp.where(s1 == v1[a:a + 1], tsel[a:a + 1], t2)
        e1_ref[h, c] = jnp.exp(s1 - v1[0:1])
        t2_ref[h, c] = t2
        e2_ref[h, c] = jnp.exp(s2 - v2[0:1]) * (0.5 / zsum)

    def body(t, carry):
        one(t // n_chunks, t % n_chunks)
        return carry

    lax.fori_loop(0, PEER_HEADS * n_chunks, body, 0)


def _peer_gate(scores, *, tm):
    nhp, n_chunks, nk, _ = scores.shape
    cpt = tm // LANES
    out = [jax.ShapeDtypeStruct((nhp // 2, n_chunks, nk, LANES), F32)] * 3
    spec = pl.BlockSpec((nhp // 2, cpt, nk, LANES), lambda i: (0, i, 0, 0))
    return pl.pallas_call(
        _peer_gate_kernel,
        grid=(n_chunks // cpt,),
        in_specs=[pl.BlockSpec((nhp, cpt, nk, LANES), lambda i: (0, i, 0, 0))],
        out_specs=[spec, spec, spec],
        out_shape=out,
        compiler_params=_cparams(("arbitrary",)),
        name="peer_gate",
    )(scores)


def _peer_dense_kernel(ft_ref, u_ref, vt_ref, e1_ref, t2_ref, e2_ref, s2_ref, o_ref,
                       h0_ref, h1_ref, w0_ref, w1_ref, *, n_blocks):
    s = pl.program_id(1)
    n_chunks, eb, _ = w0_ref.shape
    per_block = eb // N_KEYS

    @pl.when(s == 0)
    def _():
        o_ref[...] = jnp.zeros_like(o_ref)
        for ref in (h0_ref, h1_ref, w0_ref, w1_ref):
            ref[...] = jnp.zeros_like(ref)

    def stage(h_new, h_old, w_new, w_old, second):
        blk = s - 1
        live = jnp.where((blk >= 0) & (blk < n_blocks), 1.0, 0.0)
        pair = jnp.clip(blk // 2, 0, n_blocks // 2 - 1)
        hm = eb // HIDDEN_ROW_PIECES

        def hidden_piece(m0, c0, k0, nk):
            res = jnp.dot(u_ref[m0:m0 + hm, k0:k0 + nk],
                          ft_ref[k0:k0 + nk, c0 * LANES:(c0 + 2) * LANES],
                          preferred_element_type=F32)
            if k0 == 0:
                h_new[c0, m0:m0 + hm, :] = res[:, :LANES]
                h_new[c0 + 1, m0:m0 + hm, :] = res[:, LANES:]
            else:
                h_new[c0, m0:m0 + hm, :] += res[:, :LANES]
                h_new[c0 + 1, m0:m0 + hm, :] += res[:, LANES:]

        def out_piece(r0, nr):
            w = jnp.concatenate([w_old[c] for c in range(n_chunks)], axis=1)
            o_ref[r0:r0 + nr, :] += jnp.dot(vt_ref[r0:r0 + nr, :], w, preferred_element_type=F32)

        def gate_tile(a, c):
            r = per_block * second + a
            rows = slice(a * N_KEYS, (a + 1) * N_KEYS)
            gate = jnp.zeros((N_KEYS, LANES), F32)
            for h in range(PEER_HEADS):
                t2 = t2_ref[h, c, pair, r:r + 1, :]
                e1 = e1_ref[h, c, pair, r:r + 1, :] * live
                gate += jnp.where(s2_ref[h, c] >= t2, e2_ref[h, c] * e1, 0.0)
            x = h_old[c, rows, :]
            gelu2 = x * (1.0 + lax.erf(x * (2.0 ** -0.5)))
            w_new[c, rows, :] = (gate * gelu2).astype(BF16)

        d = o_ref.shape[0]
        nk = d // HIDDEN_K_PIECES
        nr = d // OUT_ROW_PIECES
        pieces = [functools.partial(hidden_piece, m0, c0, k0, nk) for k0 in range(0, d, nk)
                  for m0 in range(0, eb, hm) for c0 in range(0, n_chunks, 2)]
        pieces += [functools.partial(out_piece, r0, nr) for r0 in range(0, d, nr)]
        tiles = [(a, c) for a in range(per_block) for c in range(n_chunks)]
        done = 0
        for q, piece in enumerate(pieces):
            piece()
            want = (q + 1) * len(tiles) // len(pieces)
            for a, c in tiles[done:want]:
                gate_tile(a, c)
            done = want

    @pl.when(s % 2 == 0)
    def _():
        stage(h0_ref, h1_ref, w1_ref, w0_ref, 1)

    @pl.when(s % 2 == 1)
    def _():
        stage(h1_ref, h0_ref, w0_ref, w1_ref, 0)


def _peer_dense(f_t, u_bf, v_t, e1, t2, e2, scores, *, t, eb):
    d, n = f_t.shape
    ne = u_bf.shape[0]
    assert ne % (2 * eb) == 0 and 2 * eb == 8 * N_KEYS
    assert t % (2 * LANES) == 0
    n_blocks = ne // eb
    cpt = t // LANES
    once = dict(pipeline_mode=pl.Buffered(1))
    gspec = pl.BlockSpec((PEER_HEADS, cpt, N_KEYS, LANES), lambda i, s: (0, i, 0, 0), **once)
    pspec = pl.BlockSpec((PEER_HEADS, cpt, N_KEYS // 8, 8, LANES), lambda i, s: (0, i, 0, 0, 0),
                         **once)
    s2spec = pl.BlockSpec((PEER_HEADS, None, cpt, N_KEYS, LANES), lambda i, s: (0, 1, i, 0, 0),
                          **once)
    e1 = e1.reshape(PEER_HEADS, n // LANES, N_KEYS // 8, 8, LANES)
    t2 = t2.reshape(PEER_HEADS, n // LANES, N_KEYS // 8, 8, LANES)
    scores = scores.reshape(PEER_HEADS, 2, n // LANES, N_KEYS, LANES)
    kern = functools.partial(_peer_dense_kernel, n_blocks=n_blocks)
    return pl.pallas_call(
        kern,
        grid=(n // t, n_blocks + 2),
        in_specs=[pl.BlockSpec((d, t), lambda i, s: (0, i), **once),
                  pl.BlockSpec((eb, d), lambda i, s: (jnp.minimum(s, n_blocks - 1), 0)),
                  pl.BlockSpec((None, d, eb), lambda i, s: (jnp.clip(s - 2, 0, n_blocks - 1), 0, 0)),
                  pspec, pspec, gspec, s2spec],
        out_specs=pl.BlockSpec((d, t), lambda i, s: (0, i)),
        out_shape=jax.ShapeDtypeStruct((d, n), F32),
        scratch_shapes=[pltpu.VMEM((cpt, eb, LANES), F32), pltpu.VMEM((cpt, eb, LANES), F32),
                        pltpu.VMEM((cpt, eb, LANES), BF16), pltpu.VMEM((cpt, eb, LANES), BF16)],
        compiler_params=_cparams(("arbitrary", "arbitrary")),
        name="peer_dense",
    )(f_t, u_bf, v_t, e1, t2, e2, scores)


def _ple_kernel(h1_ref, peer_ref, p_ref, pn_ref, wg_ref, wp_ref, fin_ref, o_ref, *, last_layer):
    h2 = h1_ref[...] + peer_ref[...].T
    gate = jax.nn.sigmoid(jnp.dot(_rms(h2, pn_ref[...]).astype(BF16), wg_ref[...],
                                  preferred_element_type=F32))
    proj = jnp.dot(p_ref[...].astype(BF16), wp_ref[...], preferred_element_type=F32)
    h3 = h2 + proj * gate
    o_ref[...] = _rms(h3, fin_ref[...]) if last_layer else h3


def _ple_final(h1, peer, p2, ple_norm, w_gate, w_proj, final_norm, *, tm, last_layer):
    n, d = h1.shape
    tok = lambda w: pl.BlockSpec((tm, w), lambda i: (i, 0))
    return pl.pallas_call(
        functools.partial(_ple_kernel, last_layer=last_layer),
        grid=(n // tm,),
        in_specs=[tok(d), pl.BlockSpec((d, tm), lambda i: (0, i)), tok(p2.shape[1]),
                  _resident((1, d)), _resident(w_gate.shape),
                  _resident(w_proj.shape), _resident((1, d))],
        out_specs=tok(d),
        out_shape=jax.ShapeDtypeStruct((n, d), F32),
        compiler_params=_cparams(("arbitrary",)),
        name="ple_final",
    )(h1, peer, p2, ple_norm, w_gate, w_proj, final_norm)


def _swap_halves(w):
    half = w.shape[-1] // 2
    return jnp.concatenate([w[..., half:], w[..., :half]], axis=-1)


def _tile(n, pref):
    t = min(n, pref)
    assert n % t == 0
    return t


def kernel(x, p, positions, attn_norm, w_in, conv_w, q_norm, w_uq, kv_norm, w_ukv, conv_out_norm,
           attn_out_norm, w_out, ffn_norm, w_pq, sub_keys, u_tab, v_tab, ple_norm, w_ple_gate,
           w_ple_proj, final_norm):
    batch, seq, d = x.shape
    n = batch * seq
    depth = w_in.shape[0]
    row = lambda g: g.reshape(1, -1)

    inv_freq = ROPE_THETA ** (-(jnp.arange(0, QK_ROPE, 2, dtype=F32) / QK_ROPE))
    zeros = jnp.zeros((LANES - QK_ROPE,), F32)
    freq_row = row(jnp.concatenate([inv_freq, inv_freq, zeros]))
    sign_row = row(jnp.concatenate([-jnp.ones_like(inv_freq), jnp.ones_like(inv_freq), zeros]))
    pos2 = positions.reshape(n, 1)

    h = x.reshape(n, d)
    for i in range(depth):
        kr_cols = w_in[i][:, -QK_ROPE:]
        w_in_ext = jnp.concatenate([w_in[i], _swap_halves(kr_cols)], axis=-1).astype(BF16)
        uq = w_uq[i].reshape(-1, N_HEADS, QK_NOPE + QK_ROPE)
        uq_rope = uq[..., QK_NOPE:]
        w_uq_ext = jnp.concatenate([uq, _swap_halves(uq_rope)], axis=-1)
        w_uq_ext = w_uq_ext.reshape(-1, N_HEADS * HEAD_PAD).astype(BF16)
        ukv = w_ukv[i].reshape(-1, N_HEADS, QK_NOPE + V_HEAD)
        w_ukv_p = jnp.concatenate([ukv[..., :QK_NOPE].reshape(-1, N_HEADS * QK_NOPE),
                                   ukv[..., QK_NOPE:].reshape(-1, N_HEADS * V_HEAD)],
                                  axis=-1).astype(BF16)
        keys = sub_keys[i].reshape(-1, N_KEYS, sub_keys.shape[-1]).astype(BF16)

        conv_n, q_pad, k_pad, v = _in_proj(
            h, pos2, row(attn_norm[i]), w_in_ext, conv_w[i], row(q_norm[i]), w_uq_ext,
            row(kv_norm[i]), w_ukv_p, row(conv_out_norm[i]), freq_row, sign_row,
            seq=seq, tm=_tile(seq, 512))
        attn_n = _attention(q_pad, k_pad, v, row(attn_out_norm[i]), batch=batch, seq=seq,
                            tq=_tile(seq, 2048), tk=_tile(seq, 1024), rc=_tile(seq, 1024))
        h1, f_t, scores = _out_proj(conv_n, attn_n, h, w_out[i].astype(BF16), row(ffn_norm[i]),
                                    w_pq[i].T.astype(BF16), keys, tm=_tile(n, 512))
        e1, t2, e2 = _peer_gate(scores, tm=_tile(n, 512))
        eb = 512
        v_blocks = v_tab[i].reshape(-1, eb, d).transpose(0, 2, 1).astype(BF16)
        peer = _peer_dense(f_t, u_tab[i].astype(BF16), v_blocks, e1, t2, e2, scores,
                           t=_tile(n, 1024), eb=eb)
        h = _ple_final(h1, peer, p[i].reshape(n, -1), row(ple_norm[i]),
                       w_ple_gate[i].astype(BF16), w_ple_proj[i].astype(BF16),
                       row(final_norm), tm=_tile(n, 512), last_layer=i == depth - 1)
    return h.reshape(batch, seq, d)
```

```python
import functools

import jax
import jax.numpy as jnp
from jax import lax
from jax.experimental import pallas as pl
from jax.experimental.pallas import tpu as pltpu

EPS = 1e-6
ROPE_THETA = 10000.0
CONV_GROUP_DIM = 128
N_HEADS = 8
QK_NOPE = 128
QK_ROPE = 64
V_HEAD = 128
HEAD_PAD = 256
PEER_HEADS = 8
PEER_TOPK = 16
N_KEYS = 128
LANES = 128
HIDDEN_K_PIECES = 4
HIDDEN_ROW_PIECES = 1
OUT_ROW_PIECES = 16
NEG_BIG = -1e30

VMEM_LIMIT = 56 * 1024 * 1024

F32 = jnp.float32
BF16 = jnp.bfloat16


def _cparams(semantics):
    return pltpu.CompilerParams(dimension_semantics=semantics, vmem_limit_bytes=VMEM_LIMIT)


def _resident(shape):
    nd = len(shape)
    return pl.BlockSpec(shape, lambda *_: (0,) * nd, pipeline_mode=pl.Buffered(1))


def _rms(xf, g):
    return xf * lax.rsqrt(jnp.mean(xf * xf, axis=-1, keepdims=True) + EPS) * g


def _in_proj_kernel(x_ref, pos_ref, an_ref, win_ref, cw_ref, qn_ref, wuq_ref, kvn_ref, wukv_ref,
                    con_ref, freq_ref, sign_ref,
                    conv_ref, q_ref, kt_ref, v_ref, carry_ref, *, tiles_per_seq, conv_ch, q_lora,
                    kv_lora):
    i = pl.program_id(0)
    tm = x_ref.shape[0]

    @pl.when(i % tiles_per_seq == 0)
    def _():
        carry_ref[...] = jnp.zeros_like(carry_ref)

    a = _rms(x_ref[...], an_ref[...]).astype(BF16)

    xin = jnp.dot(a, win_ref[:, 0:conv_ch], preferred_element_type=F32)
    c_g = jnp.dot(a, win_ref[:, 2 * conv_ch:3 * conv_ch], preferred_element_type=F32)
    u = c_g * xin
    prev = carry_ref[...]
    row = lax.broadcasted_iota(jnp.int32, u.shape, 0)
    u1 = jnp.where(row == 0, prev[7:8], pltpu.roll(u, 1, 0))
    u2 = jnp.where(row == 0, prev[6:7], jnp.where(row == 1, prev[7:8], pltpu.roll(u, 2, 0)))
    carry_ref[...] = u[tm - 8:tm]
    y = cw_ref[2:3] * u + cw_ref[1:2] * u1 + cw_ref[0:1] * u2
    b_g = jnp.dot(a, win_ref[:, conv_ch:2 * conv_ch], preferred_element_type=F32)
    conv_o = b_g * y
    for g in range(conv_ch // CONV_GROUP_DIM):
        sl = slice(g * CONV_GROUP_DIM, (g + 1) * CONV_GROUP_DIM)
        conv_ref[:, sl] = _rms(conv_o[:, sl], con_ref[:, sl]).astype(BF16)

    m0 = 3 * conv_ch
    lat = jnp.dot(a, win_ref[:, m0:], preferred_element_type=F32)
    ang = pos_ref[...].astype(F32) * freq_ref[...]
    cos_t = jnp.cos(ang) * jnp.abs(sign_ref[...])
    sin_t = jnp.sin(ang) * sign_ref[...]

    def rope(blk):
        return blk * cos_t + pltpu.roll(blk, QK_ROPE, 1) * sin_t

    cq = _rms(lat[:, 0:q_lora], qn_ref[...]).astype(BF16)
    qf = jnp.dot(cq, wuq_ref[...], preferred_element_type=F32)
    ckv = _rms(lat[:, q_lora:q_lora + kv_lora], kvn_ref[...]).astype(BF16)
    kvf = jnp.dot(ckv, wukv_ref[...], preferred_element_type=F32)
    kr_t = rope(lat[:, q_lora + kv_lora:]).T.astype(BF16)
    for h in range(N_HEADS):
        o = h * HEAD_PAD
        q_ref[:, o:o + QK_NOPE] = qf[:, o:o + QK_NOPE].astype(BF16)
        q_ref[:, o + QK_NOPE:o + HEAD_PAD] = rope(qf[:, o + QK_NOPE:o + HEAD_PAD]).astype(BF16)
        kt_ref[o:o + QK_NOPE, :] = kvf[:, h * QK_NOPE:(h + 1) * QK_NOPE].T.astype(BF16)
        kt_ref[o + QK_NOPE:o + HEAD_PAD, :] = kr_t
    v_ref[...] = kvf[:, N_HEADS * QK_NOPE:].astype(BF16)


def _in_proj(x2, pos2, attn_norm, w_in_ext, conv_w, q_norm, w_uq_ext, kv_norm, w_ukv_p,
             conv_out_norm, freq_row, sign_row, *, seq, tm):
    n, d = x2.shape
    conv_ch = conv_w.shape[-1]
    q_lora, kv_lora = q_norm.shape[-1], kv_norm.shape[-1]
    assert seq % tm == 0 and n % seq == 0
    tok = lambda w: pl.BlockSpec((tm, w), lambda i: (i, 0))
    kern = functools.partial(_in_proj_kernel, tiles_per_seq=seq // tm, conv_ch=conv_ch,
                             q_lora=q_lora, kv_lora=kv_lora)
    return pl.pallas_call(
        kern,
        grid=(n // tm,),
        in_specs=[tok(d), tok(1), _resident((1, d)), _resident(w_in_ext.shape),
                  _resident(conv_w.shape), _resident((1, q_lora)), _resident(w_uq_ext.shape),
                  _resident((1, kv_lora)), _resident(w_ukv_p.shape), _resident((1, conv_ch)),
                  _resident((1, LANES)), _resident((1, LANES))],
        out_specs=[tok(conv_ch), tok(N_HEADS * HEAD_PAD),
                   pl.BlockSpec((N_HEADS * HEAD_PAD, tm), lambda i: (0, i)),
                   tok(N_HEADS * V_HEAD)],
        out_shape=[jax.ShapeDtypeStruct((n, conv_ch), BF16),
                   jax.ShapeDtypeStruct((n, N_HEADS * HEAD_PAD), BF16),
                   jax.ShapeDtypeStruct((N_HEADS * HEAD_PAD, n), BF16),
                   jax.ShapeDtypeStruct((n, N_HEADS * V_HEAD), BF16)],
        scratch_shapes=[pltpu.VMEM((8, conv_ch), F32)],
        compiler_params=_cparams(("arbitrary",)),
        name="in_proj",
    )(x2, pos2, attn_norm, w_in_ext, conv_w, q_norm, w_uq_ext, kv_norm, w_ukv_p, conv_out_norm,
      freq_row, sign_row)


def _attn_kernel(q_ref, kt_ref, v_ref, g_ref, o_ref, m_ref, l_ref, acc_ref, *, tk, rc, coef):
    qi = pl.program_id(2)
    tq = q_ref.shape[0]
    n_diag = tq // tk
    m_ref[...] = jnp.full_like(m_ref, NEG_BIG)
    l_ref[...] = jnp.zeros_like(l_ref)
    acc_ref[...] = jnp.zeros_like(acc_ref)

    def chunk_step(kb, r0, mask_offset):
        rows = slice(r0, r0 + rc)
        start = pl.multiple_of(kb * tk, tk)
        kt = kt_ref[:, pl.ds(start, tk)]
        v = v_ref[pl.ds(start, tk), :]
        s = jnp.dot(q_ref[rows, :], kt, preferred_element_type=F32)
        slabs = [s[:, j * LANES:(j + 1) * LANES] for j in range(tk // LANES)]
        if mask_offset is not None:
            r = lax.broadcasted_iota(jnp.int32, (rc, LANES), 0) + r0
            c = lax.broadcasted_iota(jnp.int32, (rc, LANES), 1)
            slabs = [jnp.where(c + (mask_offset + j * LANES) <= r, sl, NEG_BIG)
                     for j, sl in enumerate(slabs)]
        m_old = m_ref[rows, :]
        m_new = jnp.maximum(m_old, jnp.max(functools.reduce(jnp.maximum, slabs), axis=-1,
                                           keepdims=True))
        alpha = jnp.exp2((m_old - m_new) * coef)
        ps = [jnp.exp2((sl - m_new) * coef) for sl in slabs]
        l_ref[rows, :] = alpha * l_ref[rows, :] + jnp.sum(functools.reduce(jnp.add, ps), axis=-1,
                                                          keepdims=True)
        p = jnp.concatenate([x.astype(BF16) for x in ps], axis=1)
        acc_ref[rows, :] = alpha * acc_ref[rows, :] + jnp.dot(p, v, preferred_element_type=F32)
        m_ref[rows, :] = m_new

    n_full = qi * n_diag

    def body(kb, carry):
        for r0 in range(0, tq, rc):
            chunk_step(kb, r0, None)
        return carry

    lax.fori_loop(0, n_full, body, 0)
    for j in range(n_diag):
        for r0 in range(0, tq, rc):
            if r0 + rc - 1 < j * tk:
                continue
            needs_mask = r0 < j * tk + tk - 1
            chunk_step(n_full + j, r0, j * tk if needs_mask else None)

    o = acc_ref[...] / l_ref[...]
    o_ref[...] = _rms(o, g_ref[...]).astype(BF16)


def _attention(q_pad, kt_pad, v, attn_out_norm, *, batch, seq, tq, tk, rc):
    n = q_pad.shape[0]
    assert seq % tq == 0 and tq % tk == 0 and tq % rc == 0 and tk % LANES == 0
    nq = seq // tq
    coef = float(QK_NOPE + QK_ROPE) ** -0.5 * 1.4426950408889634
    kern = functools.partial(_attn_kernel, tk=tk, rc=rc, coef=coef)
    return pl.pallas_call(
        kern,
        grid=(batch, N_HEADS, nq),
        in_specs=[pl.BlockSpec((tq, HEAD_PAD), lambda b, h, i: (b * nq + i, h)),
                  pl.BlockSpec((HEAD_PAD, seq), lambda b, h, i: (h, b)),
                  pl.BlockSpec((seq, V_HEAD), lambda b, h, i: (b, h)),
                  pl.BlockSpec((1, V_HEAD), lambda b, h, i: (0, h))],
        out_specs=pl.BlockSpec((tq, V_HEAD), lambda b, h, i: (b * nq + i, h)),
        out_shape=jax.ShapeDtypeStruct((n, N_HEADS * V_HEAD), BF16),
        scratch_shapes=[pltpu.VMEM((tq, LANES), F32), pltpu.VMEM((tq, LANES), F32),
                        pltpu.VMEM((tq, V_HEAD), F32)],
        compiler_params=_cparams(("arbitrary", "arbitrary", "arbitrary")),
        name="attention",
    )(q_pad, kt_pad, v, attn_out_norm)


def _out_proj_kernel(conv_ref, attn_ref, x_ref, wo_ref, fn_ref, wpqt_ref, keys_ref,
                     h1_ref, ft_ref, sc_ref):
    conv_ch = conv_ref.shape[1]
    mix = jnp.dot(conv_ref[...], wo_ref[0:conv_ch, :], preferred_element_type=F32)
    mix += jnp.dot(attn_ref[...], wo_ref[conv_ch:, :], preferred_element_type=F32)
    h1 = x_ref[...] + mix
    h1_ref[...] = h1
    ft = _rms(h1, fn_ref[...]).T.astype(BF16)
    ft_ref[...] = ft
    qpt = jnp.dot(wpqt_ref[...], ft, preferred_element_type=F32).astype(BF16)
    for hp in range(keys_ref.shape[0]):
        sc = jnp.dot(keys_ref[hp], qpt[hp * N_KEYS:(hp + 1) * N_KEYS, :],
                     preferred_element_type=F32)
        for c in range(sc_ref.shape[1]):
            sc_ref[hp, c] = sc[:, c * LANES:(c + 1) * LANES]


def _out_proj(conv_n, attn_n, x2, w_out, ffn_norm, w_pq_t, keys, *, tm):
    n, d = x2.shape
    nhp = keys.shape[0]
    tok = lambda w: pl.BlockSpec((tm, w), lambda i: (i, 0))
    return pl.pallas_call(
        _out_proj_kernel,
        grid=(n // tm,),
        in_specs=[tok(conv_n.shape[1]), tok(attn_n.shape[1]), tok(d), _resident(w_out.shape),
                  _resident((1, d)), _resident(w_pq_t.shape), _resident(keys.shape)],
        out_specs=[tok(d), pl.BlockSpec((d, tm), lambda i: (0, i)),
                   pl.BlockSpec((nhp, tm // LANES, N_KEYS, LANES), lambda i: (0, i, 0, 0))],
        out_shape=[jax.ShapeDtypeStruct((n, d), F32), jax.ShapeDtypeStruct((d, n), BF16),
                   jax.ShapeDtypeStruct((nhp, n // LANES, N_KEYS, LANES), F32)],
        compiler_params=_cparams(("arbitrary",)),
        name="out_proj",
    )(conv_n, attn_n, x2, w_out, ffn_norm, w_pq_t, keys)


def _sort_network(n):
    pairs = []

    def merge(lo, m, r):
        step = 2 * r
        if step < m:
            merge(lo, m, step)
            merge(lo + r, m, step)
            pairs.extend((i, i + r) for i in range(lo + r, lo + m - r, step))
        else:
            pairs.append((lo, lo + r))

    def sort(lo, m):
        if m > 1:
            sort(lo, m // 2)
            sort(lo + m // 2, m // 2)
            merge(lo, m, 1)

    sort(0, n)
    return pairs


def _top16(s):
    k = PEER_TOPK
    c = s.shape[1]
    g = [s[8 * j:8 * j + 8, :] for j in range(s.shape[0] // 8)]
    assert len(g) == k

    def exchange(i, j):
        g[i], g[j] = jnp.maximum(g[i], g[j]), jnp.minimum(g[i], g[j])

    for i, j in _sort_network(k):
        exchange(i, j)
    for shift in (4, 2, 1):
        other = [pltpu.roll(x, shift, 0) for x in g]
        g = [jnp.maximum(g[j], other[k - 1 - j]) for j in range(k)]
        dist = k // 2
        while dist:
            for i in range(k):
                if not i & dist:
                    exchange(i, i + dist)
            dist //= 2
    rows = lax.broadcasted_iota(jnp.int32, (k, c), 0)
    vals = jnp.zeros((k, c), F32)
    for j in range(k):
        vals = jnp.where(rows == j, jnp.concatenate([g[j], g[j]], axis=0), vals)
    return vals


def _peer_gate_kernel(sc_ref, e1_ref, t2_ref, e2_ref):
    n_chunks = sc_ref.shape[1]

    def one(h, c):
        s1 = sc_ref[2 * h, c]
        s2 = sc_ref[2 * h + 1, c]
        v1 = _top16(s1)
        v2 = _top16(s2)
        cand = jnp.concatenate([v1 + v2[0:1]] + [v1[0:8] + v2[b:b + 1] for b in range(1, 8)]
                               + [v1[0:1] + v2[8:16]], axis=0)
        best = v1[0:1] + v2[0:1]
        zsum = jnp.zeros_like(best)
        for _ in range(PEER_TOPK):
            thr = jnp.max(cand, axis=0, keepdims=True)
            cand = jnp.where(cand == thr, -jnp.inf, cand)
            zsum += jnp.exp(thr - best)
        tsel = jnp.full_like(v1, jnp.inf)
        for b in range(PEER_TOPK):
            tsel = jnp.where(v1 + v2[b:b + 1] >= thr, v2[b:b + 1], tsel)
        t2 = jnp.full_like(s1, jnp.inf)
        for a in range(PEER_TOPK):
            t2 = jnp.where(s1 == v1[a:a + 1], tsel[a:a + 1], t2)
        e1_ref[h, c] = jnp.exp(s1 - v1[0:1])
        t2_ref[h, c] = t2
        e2_ref[h, c] = jnp.exp(s2 - v2[0:1]) * (0.5 / zsum)

    def body(t, carry):
        one(t // n_chunks, t % n_chunks)
        return carry

    lax.fori_loop(0, PEER_HEADS * n_chunks, body, 0)


def _peer_gate(scores, *, tm):
    nhp, n_chunks, nk, _ = scores.shape
    cpt = tm // LANES
    out = [jax.ShapeDtypeStruct((nhp // 2, n_chunks, nk, LANES), F32)] * 3
    spec = pl.BlockSpec((nhp // 2, cpt, nk, LANES), lambda i: (0, i, 0, 0))
    return pl.pallas_call(
        _peer_gate_kernel,
        grid=(n_chunks // cpt,),
        in_specs=[pl.BlockSpec((nhp, cpt, nk, LANES), lambda i: (0, i, 0, 0))],
        out_specs=[spec, spec, spec],
        out_shape=out,
        compiler_params=_cparams(("arbitrary",)),
        name="peer_gate",
    )(scores)


def _peer_dense_kernel(ft_ref, u_ref, vt_ref, e1_ref, t2_ref, e2_ref, s2_ref, o_ref,
                       h0_ref, h1_ref, w0_ref, w1_ref, *, n_blocks):
    s = pl.program_id(1)
    n_chunks, eb, _ = w0_ref.shape
    per_block = eb // N_KEYS

    @pl.when(s == 0)
    def _():
        o_ref[...] = jnp.zeros_like(o_ref)
        for ref in (h0_ref, h1_ref, w0_ref, w1_ref):
            ref[...] = jnp.zeros_like(ref)

    def stage(h_new, h_old, w_new, w_old, second):
        blk = s - 1
        live = jnp.where((blk >= 0) & (blk < n_blocks), 1.0, 0.0)
        pair = jnp.clip(blk // 2, 0, n_blocks // 2 - 1)
        hm = eb // HIDDEN_ROW_PIECES

        def hidden_piece(m0, c0, k0, nk):
            res = jnp.dot(u_ref[m0:m0 + hm, k0:k0 + nk],
                          ft_ref[k0:k0 + nk, c0 * LANES:(c0 + 2) * LANES],
                          preferred_element_type=F32)
            if k0 == 0:
                h_new[c0, m0:m0 + hm, :] = res[:, :LANES]
                h_new[c0 + 1, m0:m0 + hm, :] = res[:, LANES:]
            else:
                h_new[c0, m0:m0 + hm, :] += res[:, :LANES]
                h_new[c0 + 1, m0:m0 + hm, :] += res[:, LANES:]

        def out_piece(r0, nr):
            w = jnp.concatenate([w_old[c] for c in range(n_chunks)], axis=1)
            o_ref[r0:r0 + nr, :] += jnp.dot(vt_ref[r0:r0 + nr, :], w, preferred_element_type=F32)

        def gate_tile(a, c):
            r = per_block * second + a
            rows = slice(a * N_KEYS, (a + 1) * N_KEYS)
            gate = jnp.zeros((N_KEYS, LANES), F32)
            for h in range(PEER_HEADS):
                t2 = t2_ref[h, c, pair, r:r + 1, :]
                e1 = e1_ref[h, c, pair, r:r + 1, :] * live
                gate += jnp.where(s2_ref[h, c] >= t2, e2_ref[h, c] * e1, 0.0)
            x = h_old[c, rows, :]
            gelu2 = x * (1.0 + lax.erf(x * (2.0 ** -0.5)))
            w_new[c, rows, :] = (gate * gelu2).astype(BF16)

        d = o_ref.shape[0]
        nk = d // HIDDEN_K_PIECES
        nr = d // OUT_ROW_PIECES
        pieces = [functools.partial(hidden_piece, m0, c0, k0, nk) for k0 in range(0, d, nk)
                  for m0 in range(0, eb, hm) for c0 in range(0, n_chunks, 2)]
        pieces += [functools.partial(out_piece, r0, nr) for r0 in range(0, d, nr)]
        tiles = [(a, c) for a in range(per_block) for c in range(n_chunks)]
        done = 0
        for q, piece in enumerate(pieces):
            piece()
            want = (q + 1) * len(tiles) // len(pieces)
            for a, c in tiles[done:want]:
                gate_tile(a, c)
            done = want

    @pl.when(s % 2 == 0)
    def _():
        stage(h0_ref, h1_ref, w1_ref, w0_ref, 1)

    @pl.when(s % 2 == 1)
    def _():
        stage(h1_ref, h0_ref, w0_ref, w1_ref, 0)


def _peer_dense(f_t, u_bf, v_t, e1, t2, e2, scores, *, t, eb):
    d, n = f_t.shape
    ne = u_bf.shape[0]
    assert ne % (2 * eb) == 0 and 2 * eb == 8 * N_KEYS
    assert t % (2 * LANES) == 0
    n_blocks = ne // eb
    cpt = t // LANES
    once = dict(pipeline_mode=pl.Buffered(1))
    gspec = pl.BlockSpec((PEER_HEADS, cpt, N_KEYS, LANES), lambda i, s: (0, i, 0, 0), **once)
    pspec = pl.BlockSpec((PEER_HEADS, cpt, N_KEYS // 8, 8, LANES), lambda i, s: (0, i, 0, 0, 0),
                         **once)
    s2spec = pl.BlockSpec((PEER_HEADS, None, cpt, N_KEYS, LANES), lambda i, s: (0, 1, i, 0, 0),
                          **once)
    e1 = e1.reshape(PEER_HEADS, n // LANES, N_KEYS // 8, 8, LANES)
    t2 = t2.reshape(PEER_HEADS, n // LANES, N_KEYS // 8, 8, LANES)
    scores = scores.reshape(PEER_HEADS, 2, n // LANES, N_KEYS, LANES)
    kern = functools.partial(_peer_dense_kernel, n_blocks=n_blocks)
    return pl.pallas_call(
        kern,
        grid=(n // t, n_blocks + 2),
        in_specs=[pl.BlockSpec((d, t), lambda i, s: (0, i), **once),
                  pl.BlockSpec((eb, d), lambda i, s: (jnp.minimum(s, n_blocks - 1), 0)),
                  pl.BlockSpec((None, d, eb), lambda i, s: (jnp.clip(s - 2, 0, n_blocks - 1), 0, 0)),
                  pspec, pspec, gspec, s2spec],
        out_specs=pl.BlockSpec((d, t), lambda i, s: (0, i)),
        out_shape=jax.ShapeDtypeStruct((d, n), F32),
        scratch_shapes=[pltpu.VMEM((cpt, eb, LANES), F32), pltpu.VMEM((cpt, eb, LANES), F32),
                        pltpu.VMEM((cpt, eb, LANES), BF16), pltpu.VMEM((cpt, eb, LANES), BF16)],
        compiler_params=_cparams(("arbitrary", "arbitrary")),
        name="peer_dense",
    )(f_t, u_bf, v_t, e1, t2, e2, scores)


def _ple_kernel(h1_ref, peer_ref, p_ref, pn_ref, wg_ref, wp_ref, fin_ref, o_ref, *, last_layer):
    h2 = h1_ref[...] + peer_ref[...].T
    gate = jax.nn.sigmoid(jnp.dot(_rms(h2, pn_ref[...]).astype(BF16), wg_ref[...],
                                  preferred_element_type=F32))
    proj = jnp.dot(p_ref[...].astype(BF16), wp_ref[...], preferred_element_type=F32)
    h3 = h2 + proj * gate
    o_ref[...] = _rms(h3, fin_ref[...]) if last_layer else h3


def _ple_final(h1, peer, p2, ple_norm, w_gate, w_proj, final_norm, *, tm, last_layer):
    n, d = h1.shape
    tok = lambda w: pl.BlockSpec((tm, w), lambda i: (i, 0))
    return pl.pallas_call(
        functools.partial(_ple_kernel, last_layer=last_layer),
        grid=(n // tm,),
        in_specs=[tok(d), pl.BlockSpec((d, tm), lambda i: (0, i)), tok(p2.shape[1]),
                  _resident((1, d)), _resident(w_gate.shape),
                  _resident(w_proj.shape), _resident((1, d))],
        out_specs=tok(d),
        out_shape=jax.ShapeDtypeStruct((n, d), F32),
        compiler_params=_cparams(("arbitrary",)),
        name="ple_final",
    )(h1, peer, p2, ple_norm, w_gate, w_proj, final_norm)


def _swap_halves(w):
    half = w.shape[-1] // 2
    return jnp.concatenate([w[..., half:], w[..., :half]], axis=-1)


def _tile(n, pref):
    t = min(n, pref)
    assert n % t == 0
    return t


def _tiles(n, seq):
    return dict(
        in_proj=_tile(seq, 512),
        attn_q=_tile(seq, 2048), attn_rows=_tile(seq, 1024), attn_k=_tile(seq, 1024),
        out_proj=_tile(n, 512), peer_gate=_tile(n, 512), ple=_tile(n, 512),
        peer_tokens=_tile(n, 1024),
        peer_experts=4 * N_KEYS)


def kernel(x, p, positions, attn_norm, w_in, conv_w, q_norm, w_uq, kv_norm, w_ukv, conv_out_norm,
           attn_out_norm, w_out, ffn_norm, w_pq, sub_keys, u_tab, v_tab, ple_norm, w_ple_gate,
           w_ple_proj, final_norm):
    batch, seq, d = x.shape
    n = batch * seq
    depth = w_in.shape[0]
    tiles = _tiles(n, seq)
    row = lambda g: g.reshape(1, -1)

    inv_freq = ROPE_THETA ** (-(jnp.arange(0, QK_ROPE, 2, dtype=F32) / QK_ROPE))
    zeros = jnp.zeros((LANES - QK_ROPE,), F32)
    freq_row = row(jnp.concatenate([inv_freq, inv_freq, zeros]))
    sign_row = row(jnp.concatenate([-jnp.ones_like(inv_freq), jnp.ones_like(inv_freq), zeros]))
    pos2 = positions.reshape(n, 1)

    h = x.reshape(n, d)
    for i in range(depth):
        kr_cols = w_in[i][:, -QK_ROPE:]
        w_in_ext = jnp.concatenate([w_in[i], _swap_halves(kr_cols)], axis=-1).astype(BF16)
        uq = w_uq[i].reshape(-1, N_HEADS, QK_NOPE + QK_ROPE)
        uq_rope = uq[..., QK_NOPE:]
        w_uq_ext = jnp.concatenate([uq, _swap_halves(uq_rope)], axis=-1)
        w_uq_ext = w_uq_ext.reshape(-1, N_HEADS * HEAD_PAD).astype(BF16)
        ukv = w_ukv[i].reshape(-1, N_HEADS, QK_NOPE + V_HEAD)
        w_ukv_p = jnp.concatenate([ukv[..., :QK_NOPE].reshape(-1, N_HEADS * QK_NOPE),
                                   ukv[..., QK_NOPE:].reshape(-1, N_HEADS * V_HEAD)],
                                  axis=-1).astype(BF16)
        keys = sub_keys[i].reshape(-1, N_KEYS, sub_keys.shape[-1]).astype(BF16)

        conv_n, q_pad, k_pad, v = _in_proj(
            h, pos2, row(attn_norm[i]), w_in_ext, conv_w[i], row(q_norm[i]), w_uq_ext,
            row(kv_norm[i]), w_ukv_p, row(conv_out_norm[i]), freq_row, sign_row,
            seq=seq, tm=tiles["in_proj"])
        attn_n = _attention(q_pad, k_pad, v, row(attn_out_norm[i]), batch=batch, seq=seq,
                            tq=tiles["attn_q"], tk=tiles["attn_k"], rc=tiles["attn_rows"])
        h1, f_t, scores = _out_proj(conv_n, attn_n, h, w_out[i].astype(BF16), row(ffn_norm[i]),
                                    w_pq[i].T.astype(BF16), keys, tm=tiles["out_proj"])
        e1, t2, e2 = _peer_gate(scores, tm=tiles["peer_gate"])
        eb = tiles["peer_experts"]
        v_blocks = v_tab[i].reshape(-1, eb, d).transpose(0, 2, 1).astype(BF16)
        peer = _peer_dense(f_t, u_tab[i].astype(BF16), v_blocks, e1, t2, e2, scores,
                           t=tiles["peer_tokens"], eb=eb)
        h = _ple_final(h1, peer, p[i].reshape(n, -1), row(ple_norm[i]),
                       w_ple_gate[i].astype(BF16), w_ple_proj[i].astype(BF16),
                       row(final_norm), tm=tiles["ple"], last_layer=i == depth - 1)
    return h.reshape(batch, seq, d)
```

```python
import functools

import jax
import jax.numpy as jnp
from jax import lax
from jax.experimental import pallas as pl
from jax.experimental.pallas import tpu as pltpu

EPS = 1e-6
ROPE_THETA = 10000.0
CONV_GROUP_DIM = 128
N_HEADS = 8
QK_NOPE = 128
QK_ROPE = 64
V_HEAD = 128
HEAD_PAD = 256
PEER_HEADS = 8
PEER_TOPK = 16
N_KEYS = 128
LANES = 128
HIDDEN_K_PIECES = 4
HIDDEN_ROW_PIECES = 1
OUT_ROW_PIECES = 16
NEG_BIG = -1e30

VMEM_LIMIT = 56 * 1024 * 1024

F32 = jnp.float32
BF16 = jnp.bfloat16


def _cparams(semantics):
    return pltpu.CompilerParams(dimension_semantics=semantics, vmem_limit_bytes=VMEM_LIMIT)


def _resident(shape):
    nd = len(shape)
    return pl.BlockSpec(shape, lambda *_: (0,) * nd, pipeline_mode=pl.Buffered(1))


def _rms(xf, g):
    return xf * lax.rsqrt(jnp.mean(xf * xf, axis=-1, keepdims=True) + EPS) * g


def _in_proj_kernel(x_ref, pos_ref, an_ref, win_ref, cw_ref, qn_ref, wuq_ref, kvn_ref, wukv_ref,
                    con_ref, freq_ref, sign_ref,
                    conv_ref, q_ref, kt_ref, v_ref, carry_ref, *, tiles_per_seq, conv_ch, q_lora,
                    kv_lora):
    i = pl.program_id(0)
    tm = x_ref.shape[0]

    @pl.when(i % tiles_per_seq == 0)
    def _():
        carry_ref[...] = jnp.zeros_like(carry_ref)

    a = _rms(x_ref[...], an_ref[...]).astype(BF16)

    xin = jnp.dot(a, win_ref[:, 0:conv_ch], preferred_element_type=F32)
    c_g = jnp.dot(a, win_ref[:, 2 * conv_ch:3 * conv_ch], preferred_element_type=F32)
    u = c_g * xin
    prev = carry_ref[...]
    row = lax.broadcasted_iota(jnp.int32, u.shape, 0)
    u1 = jnp.where(row == 0, prev[7:8], pltpu.roll(u, 1, 0))
    u2 = jnp.where(row == 0, prev[6:7], jnp.where(row == 1, prev[7:8], pltpu.roll(u, 2, 0)))
    carry_ref[...] = u[tm - 8:tm]
    y = cw_ref[2:3] * u + cw_ref[1:2] * u1 + cw_ref[0:1] * u2
    b_g = jnp.dot(a, win_ref[:, conv_ch:2 * conv_ch], preferred_element_type=F32)
    conv_o = b_g * y
    for g in range(conv_ch // CONV_GROUP_DIM):
        sl = slice(g * CONV_GROUP_DIM, (g + 1) * CONV_GROUP_DIM)
        conv_ref[:, sl] = _rms(conv_o[:, sl], con_ref[:, sl]).astype(BF16)

    m0 = 3 * conv_ch
    lat = jnp.dot(a, win_ref[:, m0:], preferred_element_type=F32)
    ang = pos_ref[...].astype(F32) * freq_ref[...]
    cos_t = jnp.cos(ang) * jnp.abs(sign_ref[...])
    sin_t = jnp.sin(ang) * sign_ref[...]

    def rope(blk):
        return blk * cos_t + pltpu.roll(blk, QK_ROPE, 1) * sin_t

    cq = _rms(lat[:, 0:q_lora], qn_ref[...]).astype(BF16)
    qf = jnp.dot(cq, wuq_ref[...], preferred_element_type=F32)
    ckv = _rms(lat[:, q_lora:q_lora + kv_lora], kvn_ref[...]).astype(BF16)
    kvf = jnp.dot(ckv, wukv_ref[...], preferred_element_type=F32)
    kr_t = rope(lat[:, q_lora + kv_lora:]).T.astype(BF16)
    for h in range(N_HEADS):
        o = h * HEAD_PAD
        q_ref[:, o:o + QK_NOPE] = qf[:, o:o + QK_NOPE].astype(BF16)
        q_ref[:, o + QK_NOPE:o + HEAD_PAD] = rope(qf[:, o + QK_NOPE:o + HEAD_PAD]).astype(BF16)
        kt_ref[o:o + QK_NOPE, :] = kvf[:, h * QK_NOPE:(h + 1) * QK_NOPE].T.astype(BF16)
        kt_ref[o + QK_NOPE:o + HEAD_PAD, :] = kr_t
    v_ref[...] = kvf[:, N_HEADS * QK_NOPE:].astype(BF16)


def _in_proj(x2, pos2, attn_norm, w_in_ext, conv_w, q_norm, w_uq_ext, kv_norm, w_ukv_p,
             conv_out_norm, freq_row, sign_row, *, seq, tm):
    n, d = x2.shape
    conv_ch = conv_w.shape[-1]
    q_lora, kv_lora = q_norm.shape[-1], kv_norm.shape[-1]
    assert seq % tm == 0 and n % seq == 0
    tok = lambda w: pl.BlockSpec((tm, w), lambda i: (i, 0))
    kern = functools.partial(_in_proj_kernel, tiles_per_seq=seq // tm, conv_ch=conv_ch,
                             q_lora=q_lora, kv_lora=kv_lora)
    return pl.pallas_call(
        kern,
        grid=(n // tm,),
        in_specs=[tok(d), tok(1), _resident((1, d)), _resident(w_in_ext.shape),
                  _resident(conv_w.shape), _resident((1, q_lora)), _resident(w_uq_ext.shape),
                  _resident((1, kv_lora)), _resident(w_ukv_p.shape), _resident((1, conv_ch)),
                  _resident((1, LANES)), _resident((1, LANES))],
        out_specs=[tok(conv_ch), tok(N_HEADS * HEAD_PAD),
                   pl.BlockSpec((N_HEADS * HEAD_PAD, tm), lambda i: (0, i)),
                   tok(N_HEADS * V_HEAD)],
        out_shape=[jax.ShapeDtypeStruct((n, conv_ch), BF16),
                   jax.ShapeDtypeStruct((n, N_HEADS * HEAD_PAD), BF16),
                   jax.ShapeDtypeStruct((N_HEADS * HEAD_PAD, n), BF16),
                   jax.ShapeDtypeStruct((n, N_HEADS * V_HEAD), BF16)],
        scratch_shapes=[pltpu.VMEM((8, conv_ch), F32)],
        compiler_params=_cparams(("arbitrary",)),
        name="in_proj",
    )(x2, pos2, attn_norm, w_in_ext, conv_w, q_norm, w_uq_ext, kv_norm, w_ukv_p, conv_out_norm,
      freq_row, sign_row)


def _attn_kernel(q_ref, kt_ref, v_ref, g_ref, o_ref, m_ref, l_ref, acc_ref, *, tk, rc, coef):
    qi = pl.program_id(2)
    tq = q_ref.shape[0]
    n_diag = tq // tk
    m_ref[...] = jnp.full_like(m_ref, NEG_BIG)
    l_ref[...] = jnp.zeros_like(l_ref)
    acc_ref[...] = jnp.zeros_like(acc_ref)

    def chunk_step(kb, r0, mask_offset):
        rows = slice(r0, r0 + rc)
        start = pl.multiple_of(kb * tk, tk)
        kt = kt_ref[:, pl.ds(start, tk)]
        v = v_ref[pl.ds(start, tk), :]
        s = jnp.dot(q_ref[rows, :], kt, preferred_element_type=F32)
        slabs = [s[:, j * LANES:(j + 1) * LANES] for j in range(tk // LANES)]
        if mask_offset is not None:
            r = lax.broadcasted_iota(jnp.int32, (rc, LANES), 0) + r0
            c = lax.broadcasted_iota(jnp.int32, (rc, LANES), 1)
            slabs = [jnp.where(c + (mask_offset + j * LANES) <= r, sl, NEG_BIG)
                     for j, sl in enumerate(slabs)]
        m_old = m_ref[rows, :]
        m_new = jnp.maximum(m_old, jnp.max(functools.reduce(jnp.maximum, slabs), axis=-1,
                                           keepdims=True))
        alpha = jnp.exp2((m_old - m_new) * coef)
        ps = [jnp.exp2((sl - m_new) * coef) for sl in slabs]
        l_ref[rows, :] = alpha * l_ref[rows, :] + jnp.sum(functools.reduce(jnp.add, ps), axis=-1,
                                                          keepdims=True)
        p = jnp.concatenate([x.astype(BF16) for x in ps], axis=1)
        acc_ref[rows, :] = alpha * acc_ref[rows, :] + jnp.dot(p, v, preferred_element_type=F32)
        m_ref[rows, :] = m_new

    n_full = qi * n_diag

    def body(kb, carry):
        for r0 in range(0, tq, rc):
            chunk_step(kb, r0, None)
        return carry

    lax.fori_loop(0, n_full, body, 0)
    for j in range(n_diag):
        for r0 in range(0, tq, rc):
            if r0 + rc - 1 < j * tk:
                continue
            needs_mask = r0 < j * tk + tk - 1
            chunk_step(n_full + j, r0, j * tk if needs_mask else None)

    o = acc_ref[...] / l_ref[...]
    o_ref[...] = _rms(o, g_ref[...]).astype(BF16)


def _attention(q_pad, kt_pad, v, attn_out_norm, *, batch, seq, tq, tk, rc):
    n = q_pad.shape[0]
    assert seq % tq == 0 and tq % tk == 0 and tq % rc == 0 and tk % LANES == 0
    nq = seq // tq
    coef = float(QK_NOPE + QK_ROPE) ** -0.5 * 1.4426950408889634
    kern = functools.partial(_attn_kernel, tk=tk, rc=rc, coef=coef)
    return pl.pallas_call(
        kern,
        grid=(batch, N_HEADS, nq),
        in_specs=[pl.BlockSpec((tq, HEAD_PAD), lambda b, h, i: (b * nq + i, h)),
                  pl.BlockSpec((HEAD_PAD, seq), lambda b, h, i: (h, b)),
                  pl.BlockSpec((seq, V_HEAD), lambda b, h, i: (b, h)),
                  pl.BlockSpec((1, V_HEAD), lambda b, h, i: (0, h))],
        out_specs=pl.BlockSpec((tq, V_HEAD), lambda b, h, i: (b * nq + i, h)),
        out_shape=jax.ShapeDtypeStruct((n, N_HEADS * V_HEAD), BF16),
        scratch_shapes=[pltpu.VMEM((tq, LANES), F32), pltpu.VMEM((tq, LANES), F32),
                        pltpu.VMEM((tq, V_HEAD), F32)],
        compiler_params=_cparams(("arbitrary", "arbitrary", "arbitrary")),
        name="attention",
    )(q_pad, kt_pad, v, attn_out_norm)


def _out_proj_kernel(conv_ref, attn_ref, x_ref, wo_ref, fn_ref, wpqt_ref, keys_ref,
                     h1_ref, ft_ref, sc_ref):
    conv_ch = conv_ref.shape[1]
    mix = jnp.dot(conv_ref[...], wo_ref[0:conv_ch, :], preferred_element_type=F32)
    mix += jnp.dot(attn_ref[...], wo_ref[conv_ch:, :], preferred_element_type=F32)
    h1 = x_ref[...] + mix
    h1_ref[...] = h1
    ft = _rms(h1, fn_ref[...]).T.astype(BF16)
    ft_ref[...] = ft
    qpt = jnp.dot(wpqt_ref[...], ft, preferred_element_type=F32).astype(BF16)
    for hp in range(keys_ref.shape[0]):
        sc = jnp.dot(keys_ref[hp], qpt[hp * N_KEYS:(hp + 1) * N_KEYS, :],
                     preferred_element_type=F32)
        for c in range(sc_ref.shape[1]):
            sc_ref[hp, c] = sc[:, c * LANES:(c + 1) * LANES]


def _out_proj(conv_n, attn_n, x2, w_out, ffn_norm, w_pq_t, keys, *, tm):
    n, d = x2.shape
    nhp = keys.shape[0]
    tok = lambda w: pl.BlockSpec((tm, w), lambda i: (i, 0))
    return pl.pallas_call(
        _out_proj_kernel,
        grid=(n // tm,),
        in_specs=[tok(conv_n.shape[1]), tok(attn_n.shape[1]), tok(d), _resident(w_out.shape),
                  _resident((1, d)), _resident(w_pq_t.shape), _resident(keys.shape)],
        out_specs=[tok(d), pl.BlockSpec((d, tm), lambda i: (0, i)),
                   pl.BlockSpec((nhp, tm // LANES, N_KEYS, LANES), lambda i: (0, i, 0, 0))],
        out_shape=[jax.ShapeDtypeStruct((n, d), F32), jax.ShapeDtypeStruct((d, n), BF16),
                   jax.ShapeDtypeStruct((nhp, n // LANES, N_KEYS, LANES), F32)],
        compiler_params=_cparams(("arbitrary",)),
        name="out_proj",
    )(conv_n, attn_n, x2, w_out, ffn_norm, w_pq_t, keys)


def _sort_network(n):
    pairs = []

    def merge(lo, m, r):
        step = 2 * r
        if step < m:
            merge(lo, m, step)
            merge(lo + r, m, step)
            pairs.extend((i, i + r) for i in range(lo + r, lo + m - r, step))
        else:
            pairs.append((lo, lo + r))

    def sort(lo, m):
        if m > 1:
            sort(lo, m // 2)
            sort(lo + m // 2, m // 2)
            merge(lo, m, 1)

    sort(0, n)
    return pairs


def _top16(s):
    k = PEER_TOPK
    c = s.shape[1]
    g = [s[8 * j:8 * j + 8, :] for j in range(s.shape[0] // 8)]
    assert len(g) == k

    def exchange(i, j):
        g[i], g[j] = jnp.maximum(g[i], g[j]), jnp.minimum(g[i], g[j])

    for i, j in _sort_network(k):
        exchange(i, j)
    for shift in (4, 2, 1):
        other = [pltpu.roll(x, shift, 0) for x in g]
        g = [jnp.maximum(g[j], other[k - 1 - j]) for j in range(k)]
        dist = k // 2
        while dist:
            for i in range(k):
                if not i & dist:
                    exchange(i, i + dist)
            dist //= 2
    rows = lax.broadcasted_iota(jnp.int32, (k, c), 0)
    vals = jnp.zeros((k, c), F32)
    for j in range(k):
        vals = jnp.where(rows == j, jnp.concatenate([g[j], g[j]], axis=0), vals)
    return vals


def _peer_gate_kernel(sc_ref, e1_ref, t2_ref, e2_ref):
    n_chunks = sc_ref.shape[1]

    def one(h, c):
        s1 = sc_ref[2 * h, c]
        s2 = sc_ref[2 * h + 1, c]
        v1 = _top16(s1)
        v2 = _top16(s2)
        cand = jnp.concatenate([v1 + v2[0:1]] + [v1[0:8] + v2[b:b + 1] for b in range(1, 8)]
                               + [v1[0:1] + v2[8:16]], axis=0)
        best = v1[0:1] + v2[0:1]
        pad = jnp.full((N_KEYS - cand.shape[0], cand.shape[1]), -jnp.inf, F32)
        top = _top16(jnp.concatenate([cand, pad], axis=0))
        thr = top[PEER_TOPK - 1:PEER_TOPK]
        zsum = jnp.sum(jnp.exp(top - best), axis=0, keepdims=True)
        tsel = jnp.full_like(v1, jnp.inf)
        for b in range(PEER_TOPK):
            tsel = jnp.where(v1 + v2[b:b + 1] >= thr, v2[b:b + 1], tsel)
        t2 = jnp.full_like(s1, jnp.inf)
        for a in range(PEER_TOPK):
            t2 = jnp.where(s1 == v1[a:a + 1], tsel[a:a + 1], t2)
        e1_ref[h, c] = jnp.exp(s1 - v1[0:1])
        t2_ref[h, c] = t2
        e2_ref[h, c] = jnp.exp(s2 - v2[0:1]) * (0.5 / zsum)

    def body(t, carry):
        one(t // n_chunks, t % n_chunks)
        return carry

    lax.fori_loop(0, PEER_HEADS * n_chunks, body, 0)


def _peer_gate(scores, *, tm):
    nhp, n_chunks, nk, _ = scores.shape
    cpt = tm // LANES
    out = [jax.ShapeDtypeStruct((nhp // 2, n_chunks, nk, LANES), F32)] * 3
    spec = pl.BlockSpec((nhp // 2, cpt, nk, LANES), lambda i: (0, i, 0, 0))
    return pl.pallas_call(
        _peer_gate_kernel,
        grid=(n_chunks // cpt,),
        in_specs=[pl.BlockSpec((nhp, cpt, nk, LANES), lambda i: (0, i, 0, 0))],
        out_specs=[spec, spec, spec],
        out_shape=out,
        compiler_params=_cparams(("arbitrary",)),
        name="peer_gate",
    )(scores)


def _peer_dense_kernel(ft_ref, u_ref, vt_ref, e1_ref, t2_ref, e2_ref, s2_ref, o_ref,
                       h0_ref, h1_ref, w0_ref, w1_ref, *, n_blocks):
    s = pl.program_id(1)
    n_chunks, eb, _ = w0_ref.shape
    per_block = eb // N_KEYS

    @pl.when(s == 0)
    def _():
        o_ref[...] = jnp.zeros_like(o_ref)
        for ref in (h0_ref, h1_ref, w0_ref, w1_ref):
            ref[...] = jnp.zeros_like(ref)

    def stage(h_new, h_old, w_new, w_old, second):
        blk = s - 1
        live = jnp.where((blk >= 0) & (blk < n_blocks), 1.0, 0.0)
        pair = jnp.clip(blk // 2, 0, n_blocks // 2 - 1)
        hm = eb // HIDDEN_ROW_PIECES

        def hidden_piece(m0, c0, k0, nk):
            res = jnp.dot(u_ref[m0:m0 + hm, k0:k0 + nk],
                          ft_ref[k0:k0 + nk, c0 * LANES:(c0 + 2) * LANES],
                          preferred_element_type=F32)
            if k0 == 0:
                h_new[c0, m0:m0 + hm, :] = res[:, :LANES]
                h_new[c0 + 1, m0:m0 + hm, :] = res[:, LANES:]
            else:
                h_new[c0, m0:m0 + hm, :] += res[:, :LANES]
                h_new[c0 + 1, m0:m0 + hm, :] += res[:, LANES:]

        def out_piece(r0, nr):
            w = jnp.concatenate([w_old[c] for c in range(n_chunks)], axis=1)
            o_ref[r0:r0 + nr, :] += jnp.dot(vt_ref[r0:r0 + nr, :], w, preferred_element_type=F32)

        def gate_tile(a, c):
            r = per_block * second + a
            rows = slice(a * N_KEYS, (a + 1) * N_KEYS)
            gate = jnp.zeros((N_KEYS, LANES), F32)
            for h in range(PEER_HEADS):
                t2 = t2_ref[h, c, pair, r:r + 1, :]
                e1 = e1_ref[h, c, pair, r:r + 1, :] * live
                gate += jnp.where(s2_ref[h, c] >= t2, e2_ref[h, c] * e1, 0.0)
            x = h_old[c, rows, :]
            gelu2 = x * (1.0 + lax.erf(x * (2.0 ** -0.5)))
            w_new[c, rows, :] = (gate * gelu2).astype(BF16)

        d = o_ref.shape[0]
        nk = d // HIDDEN_K_PIECES
        nr = d // OUT_ROW_PIECES
        pieces = [functools.partial(hidden_piece, m0, c0, k0, nk) for k0 in range(0, d, nk)
                  for m0 in range(0, eb, hm) for c0 in range(0, n_chunks, 2)]
        pieces += [functools.partial(out_piece, r0, nr) for r0 in range(0, d, nr)]
        tiles = [(a, c) for a in range(per_block) for c in range(n_chunks)]
        done = 0
        for q, piece in enumerate(pieces):
            piece()
            want = (q + 1) * len(tiles) // len(pieces)
            for a, c in tiles[done:want]:
                gate_tile(a, c)
            done = want

    @pl.when(s % 2 == 0)
    def _():
        stage(h0_ref, h1_ref, w1_ref, w0_ref, 1)

    @pl.when(s % 2 == 1)
    def _():
        stage(h1_ref, h0_ref, w0_ref, w1_ref, 0)


def _peer_dense(f_t, u_bf, v_t, e1, t2, e2, scores, *, t, eb):
    d, n = f_t.shape
    ne = u_bf.shape[0]
    assert ne % (2 * eb) == 0 and 2 * eb == 8 * N_KEYS
    assert t % (2 * LANES) == 0
    n_blocks = ne // eb
    cpt = t // LANES
    once = dict(pipeline_mode=pl.Buffered(1))
    gspec = pl.BlockSpec((PEER_HEADS, cpt, N_KEYS, LANES), lambda i, s: (0, i, 0, 0), **once)
    pspec = pl.BlockSpec((PEER_HEADS, cpt, N_KEYS // 8, 8, LANES), lambda i, s: (0, i, 0, 0, 0),
                         **once)
    s2spec = pl.BlockSpec((PEER_HEADS, None, cpt, N_KEYS, LANES), lambda i, s: (0, 1, i, 0, 0),
                          **once)
    e1 = e1.reshape(PEER_HEADS, n // LANES, N_KEYS // 8, 8, LANES)
    t2 = t2.reshape(PEER_HEADS, n // LANES, N_KEYS // 8, 8, LANES)
    scores = scores.reshape(PEER_HEADS, 2, n // LANES, N_KEYS, LANES)
    kern = functools.partial(_peer_dense_kernel, n_blocks=n_blocks)
    return pl.pallas_call(
        kern,
        grid=(n // t, n_blocks + 2),
        in_specs=[pl.BlockSpec((d, t), lambda i, s: (0, i), **once),
                  pl.BlockSpec((eb, d), lambda i, s: (jnp.minimum(s, n_blocks - 1), 0)),
                  pl.BlockSpec((None, d, eb), lambda i, s: (jnp.clip(s - 2, 0, n_blocks - 1), 0, 0)),
                  pspec, pspec, gspec, s2spec],
        out_specs=pl.BlockSpec((d, t), lambda i, s: (0, i)),
        out_shape=jax.ShapeDtypeStruct((d, n), F32),
        scratch_shapes=[pltpu.VMEM((cpt, eb, LANES), F32), pltpu.VMEM((cpt, eb, LANES), F32),
                        pltpu.VMEM((cpt, eb, LANES), BF16), pltpu.VMEM((cpt, eb, LANES), BF16)],
        compiler_params=_cparams(("arbitrary", "arbitrary")),
        name="peer_dense",
    )(f_t, u_bf, v_t, e1, t2, e2, scores)


def _ple_kernel(h1_ref, peer_ref, p_ref, pn_ref, wg_ref, wp_ref, fin_ref, o_ref, *, last_layer):
    h2 = h1_ref[...] + peer_ref[...].T
    gate = jax.nn.sigmoid(jnp.dot(_rms(h2, pn_ref[...]).astype(BF16), wg_ref[...],
                                  preferred_element_type=F32))
    proj = jnp.dot(p_ref[...].astype(BF16), wp_ref[...], preferred_element_type=F32)
    h3 = h2 + proj * gate
    o_ref[...] = _rms(h3, fin_ref[...]) if last_layer else h3


def _ple_final(h1, peer, p2, ple_norm, w_gate, w_proj, final_norm, *, tm, last_layer):
    n, d = h1.shape
    tok = lambda w: pl.BlockSpec((tm, w), lambda i: (i, 0))
    return pl.pallas_call(
        functools.partial(_ple_kernel, last_layer=last_layer),
        grid=(n // tm,),
        in_specs=[tok(d), pl.BlockSpec((d, tm), lambda i: (0, i)), tok(p2.shape[1]),
                  _resident((1, d)), _resident(w_gate.shape),
                  _resident(w_proj.shape), _resident((1, d))],
        out_specs=tok(d),
        out_shape=jax.ShapeDtypeStruct((n, d), F32),
        compiler_params=_cparams(("arbitrary",)),
        name="ple_final",
    )(h1, peer, p2, ple_norm, w_gate, w_proj, final_norm)


def _swap_halves(w):
    half = w.shape[-1] // 2
    return jnp.concatenate([w[..., half:], w[..., :half]], axis=-1)


def _tile(n, pref):
    t = min(n, pref)
    assert n % t == 0
    return t


def _tiles(n, seq):
    return dict(
        in_proj=_tile(seq, 512),
        attn_q=_tile(seq, 2048), attn_rows=_tile(seq, 1024), attn_k=_tile(seq, 1024),
        out_proj=_tile(n, 512), peer_gate=_tile(n, 512), ple=_tile(n, 512),
        peer_tokens=_tile(n, 1024),
        peer_experts=4 * N_KEYS)


def kernel(x, p, positions, attn_norm, w_in, conv_w, q_norm, w_uq, kv_norm, w_ukv, conv_out_norm,
           attn_out_norm, w_out, ffn_norm, w_pq, sub_keys, u_tab, v_tab, ple_norm, w_ple_gate,
           w_ple_proj, final_norm):
    batch, seq, d = x.shape
    n = batch * seq
    depth = w_in.shape[0]
    tiles = _tiles(n, seq)
    row = lambda g: g.reshape(1, -1)

    inv_freq = ROPE_THETA ** (-(jnp.arange(0, QK_ROPE, 2, dtype=F32) / QK_ROPE))
    zeros = jnp.zeros((LANES - QK_ROPE,), F32)
    freq_row = row(jnp.concatenate([inv_freq, inv_freq, zeros]))
    sign_row = row(jnp.concatenate([-jnp.ones_like(inv_freq), jnp.ones_like(inv_freq), zeros]))
    pos2 = positions.reshape(n, 1)

    h = x.reshape(n, d)
    for i in range(depth):
        kr_cols = w_in[i][:, -QK_ROPE:]
        w_in_ext = jnp.concatenate([w_in[i], _swap_halves(kr_cols)], axis=-1).astype(BF16)
        uq = w_uq[i].reshape(-1, N_HEADS, QK_NOPE + QK_ROPE)
        uq_rope = uq[..., QK_NOPE:]
        w_uq_ext = jnp.concatenate([uq, _swap_halves(uq_rope)], axis=-1)
        w_uq_ext = w_uq_ext.reshape(-1, N_HEADS * HEAD_PAD).astype(BF16)
        ukv = w_ukv[i].reshape(-1, N_HEADS, QK_NOPE + V_HEAD)
        w_ukv_p = jnp.concatenate([ukv[..., :QK_NOPE].reshape(-1, N_HEADS * QK_NOPE),
                                   ukv[..., QK_NOPE:].reshape(-1, N_HEADS * V_HEAD)],
                                  axis=-1).astype(BF16)
        keys = sub_keys[i].reshape(-1, N_KEYS, sub_keys.shape[-1]).astype(BF16)

        conv_n, q_pad, k_pad, v = _in_proj(
            h, pos2, row(attn_norm[i]), w_in_ext, conv_w[i], row(q_norm[i]), w_uq_ext,
            row(kv_norm[i]), w_ukv_p, row(conv_out_norm[i]), freq_row, sign_row,
            seq=seq, tm=tiles["in_proj"])
        attn_n = _attention(q_pad, k_pad, v, row(attn_out_norm[i]), batch=batch, seq=seq,
                            tq=tiles["attn_q"], tk=tiles["attn_k"], rc=tiles["attn_rows"])
        h1, f_t, scores = _out_proj(conv_n, attn_n, h, w_out[i].astype(BF16), row(ffn_norm[i]),
                                    w_pq[i].T.astype(BF16), keys, tm=tiles["out_proj"])
        e1, t2, e2 = _peer_gate(scores, tm=tiles["peer_gate"])
        eb = tiles["peer_experts"]
        v_blocks = v_tab[i].reshape(-1, eb, d).transpose(0, 2, 1).astype(BF16)
        peer = _peer_dense(f_t, u_tab[i].astype(BF16), v_blocks, e1, t2, e2, scores,
                           t=tiles["peer_tokens"], eb=eb)
        h = _ple_final(h1, peer, p[i].reshape(n, -1), row(ple_norm[i]),
                       w_ple_gate[i].astype(BF16), w_ple_proj[i].astype(BF16),
                       row(final_norm), tm=tiles["ple"], last_layer=i == depth - 1)
    return h.reshape(batch, seq, d)
```

```python
import functools

import jax
import jax.numpy as jnp
from jax import lax
from jax.experimental import pallas as pl
from jax.experimental.pallas import tpu as pltpu

EPS = 1e-6
ROPE_THETA = 10000.0
CONV_GROUP_DIM = 128
N_HEADS = 8
QK_NOPE = 128
QK_ROPE = 64
V_HEAD = 128
HEAD_PAD = 256
PEER_HEADS = 8
PEER_TOPK = 16
N_KEYS = 128
LANES = 128
HIDDEN_K_PIECES = 4
HIDDEN_ROW_PIECES = 1
OUT_ROW_PIECES = 16
NEG_BIG = -1e30

VMEM_LIMIT = 56 * 1024 * 1024

F32 = jnp.float32
BF16 = jnp.bfloat16


def _cparams(semantics):
    return pltpu.CompilerParams(dimension_semantics=semantics, vmem_limit_bytes=VMEM_LIMIT)


def _resident(shape):
    nd = len(shape)
    return pl.BlockSpec(shape, lambda *_: (0,) * nd, pipeline_mode=pl.Buffered(1))


def _rms(xf, g):
    return xf * lax.rsqrt(jnp.mean(xf * xf, axis=-1, keepdims=True) + EPS) * g


def _in_proj_kernel(x_ref, pos_ref, an_ref, win_ref, cw_ref, qn_ref, wuq_ref, kvn_ref, wukv_ref,
                    con_ref, freq_ref, sign_ref,
                    conv_ref, q_ref, kt_ref, v_ref, carry_ref, *, tiles_per_seq, conv_ch, q_lora,
                    kv_lora):
    i = pl.program_id(0)
    tm = x_ref.shape[0]

    @pl.when(i % tiles_per_seq == 0)
    def _():
        carry_ref[...] = jnp.zeros_like(carry_ref)

    a = _rms(x_ref[...], an_ref[...]).astype(BF16)

    xin = jnp.dot(a, win_ref[:, 0:conv_ch], preferred_element_type=F32)
    c_g = jnp.dot(a, win_ref[:, 2 * conv_ch:3 * conv_ch], preferred_element_type=F32)
    u = c_g * xin
    prev = carry_ref[...]
    row = lax.broadcasted_iota(jnp.int32, u.shape, 0)
    u1 = jnp.where(row == 0, prev[7:8], pltpu.roll(u, 1, 0))
    u2 = jnp.where(row == 0, prev[6:7], jnp.where(row == 1, prev[7:8], pltpu.roll(u, 2, 0)))
    carry_ref[...] = u[tm - 8:tm]
    y = cw_ref[2:3] * u + cw_ref[1:2] * u1 + cw_ref[0:1] * u2
    b_g = jnp.dot(a, win_ref[:, conv_ch:2 * conv_ch], preferred_element_type=F32)
    conv_o = b_g * y
    for g in range(conv_ch // CONV_GROUP_DIM):
        sl = slice(g * CONV_GROUP_DIM, (g + 1) * CONV_GROUP_DIM)
        conv_ref[:, sl] = _rms(conv_o[:, sl], con_ref[:, sl]).astype(BF16)

    m0 = 3 * conv_ch
    lat = jnp.dot(a, win_ref[:, m0:], preferred_element_type=F32)
    ang = pos_ref[...].astype(F32) * freq_ref[...]
    cos_t = jnp.cos(ang) * jnp.abs(sign_ref[...])
    sin_t = jnp.sin(ang) * sign_ref[...]

    def rope(blk):
        return blk * cos_t + pltpu.roll(blk, QK_ROPE, 1) * sin_t

    cq = _rms(lat[:, 0:q_lora], qn_ref[...]).astype(BF16)
    qf = jnp.dot(cq, wuq_ref[...], preferred_element_type=F32)
    ckv = _rms(lat[:, q_lora:q_lora + kv_lora], kvn_ref[...]).astype(BF16)
    kvf = jnp.dot(ckv, wukv_ref[...], preferred_element_type=F32)
    kr_t = rope(lat[:, q_lora + kv_lora:]).T.astype(BF16)
    for h in range(N_HEADS):
        o = h * HEAD_PAD
        q_ref[:, o:o + QK_NOPE] = qf[:, o:o + QK_NOPE].astype(BF16)
        q_ref[:, o + QK_NOPE:o + HEAD_PAD] = rope(qf[:, o + QK_NOPE:o + HEAD_PAD]).astype(BF16)
        kt_ref[o:o + QK_NOPE, :] = kvf[:, h * QK_NOPE:(h + 1) * QK_NOPE].T.astype(BF16)
        kt_ref[o + QK_NOPE:o + HEAD_PAD, :] = kr_t
    v_ref[...] = kvf[:, N_HEADS * QK_NOPE:].astype(BF16)


def _in_proj(x2, pos2, attn_norm, w_in_ext, conv_w, q_norm, w_uq_ext, kv_norm, w_ukv_p,
             conv_out_norm, freq_row, sign_row, *, seq, tm):
    n, d = x2.shape
    conv_ch = conv_w.shape[-1]
    q_lora, kv_lora = q_norm.shape[-1], kv_norm.shape[-1]
    assert seq % tm == 0 and n % seq == 0
    tok = lambda w: pl.BlockSpec((tm, w), lambda i: (i, 0))
    kern = functools.partial(_in_proj_kernel, tiles_per_seq=seq // tm, conv_ch=conv_ch,
                             q_lora=q_lora, kv_lora=kv_lora)
    return pl.pallas_call(
        kern,
        grid=(n // tm,),
        in_specs=[tok(d), tok(1), _resident((1, d)), _resident(w_in_ext.shape),
                  _resident(conv_w.shape), _resident((1, q_lora)), _resident(w_uq_ext.shape),
                  _resident((1, kv_lora)), _resident(w_ukv_p.shape), _resident((1, conv_ch)),
                  _resident((1, LANES)), _resident((1, LANES))],
        out_specs=[tok(conv_ch), tok(N_HEADS * HEAD_PAD),
                   pl.BlockSpec((N_HEADS * HEAD_PAD, tm), lambda i: (0, i)),
                   tok(N_HEADS * V_HEAD)],
        out_shape=[jax.ShapeDtypeStruct((n, conv_ch), BF16),
                   jax.ShapeDtypeStruct((n, N_HEADS * HEAD_PAD), BF16),
                   jax.ShapeDtypeStruct((N_HEADS * HEAD_PAD, n), BF16),
                   jax.ShapeDtypeStruct((n, N_HEADS * V_HEAD), BF16)],
        scratch_shapes=[pltpu.VMEM((8, conv_ch), F32)],
        compiler_params=_cparams(("arbitrary",)),
        name="in_proj",
    )(x2, pos2, attn_norm, w_in_ext, conv_w, q_norm, w_uq_ext, kv_norm, w_ukv_p, conv_out_norm,
      freq_row, sign_row)


def _attn_kernel(q_ref, kt_ref, v_ref, g_ref, o_ref, m_ref, l_ref, acc_ref, *, tk, rc, coef):
    qi = pl.program_id(2)
    tq = q_ref.shape[0]
    n_diag = tq // tk
    m_ref[...] = jnp.full_like(m_ref, NEG_BIG)
    l_ref[...] = jnp.zeros_like(l_ref)
    acc_ref[...] = jnp.zeros_like(acc_ref)

    def chunk_step(kb, r0, mask_offset):
        rows = slice(r0, r0 + rc)
        start = pl.multiple_of(kb * tk, tk)
        kt = kt_ref[:, pl.ds(start, tk)]
        v = v_ref[pl.ds(start, tk), :]
        s = jnp.dot(q_ref[rows, :], kt, preferred_element_type=F32)
        slabs = [s[:, j * LANES:(j + 1) * LANES] for j in range(tk // LANES)]
        if mask_offset is not None:
            r = lax.broadcasted_iota(jnp.int32, (rc, LANES), 0) + r0
            c = lax.broadcasted_iota(jnp.int32, (rc, LANES), 1)
            slabs = [jnp.where(c + (mask_offset + j * LANES) <= r, sl, NEG_BIG)
                     for j, sl in enumerate(slabs)]
        m_old = m_ref[rows, :]
        m_new = jnp.maximum(m_old, jnp.max(functools.reduce(jnp.maximum, slabs), axis=-1,
                                           keepdims=True))
        alpha = jnp.exp2((m_old - m_new) * coef)
        ps = [jnp.exp2((sl - m_new) * coef) for sl in slabs]
        l_ref[rows, :] = alpha * l_ref[rows, :] + jnp.sum(functools.reduce(jnp.add, ps), axis=-1,
                                                          keepdims=True)
        p = jnp.concatenate([x.astype(BF16) for x in ps], axis=1)
        acc_ref[rows, :] = alpha * acc_ref[rows, :] + jnp.dot(p, v, preferred_element_type=F32)
        m_ref[rows, :] = m_new

    n_full = qi * n_diag

    def body(kb, carry):
        for r0 in range(0, tq, rc):
            chunk_step(kb, r0, None)
        return carry

    lax.fori_loop(0, n_full, body, 0)
    for j in range(n_diag):
        for r0 in range(0, tq, rc):
            if r0 + rc - 1 < j * tk:
                continue
            needs_mask = r0 < j * tk + tk - 1
            chunk_step(n_full + j, r0, j * tk if needs_mask else None)

    o = acc_ref[...] / l_ref[...]
    o_ref[...] = _rms(o, g_ref[...]).astype(BF16)


def _attention(q_pad, kt_pad, v, attn_out_norm, *, batch, seq, tq, tk, rc):
    n = q_pad.shape[0]
    assert seq % tq == 0 and tq % tk == 0 and tq % rc == 0 and tk % LANES == 0
    nq = seq // tq
    coef = float(QK_NOPE + QK_ROPE) ** -0.5 * 1.4426950408889634
    kern = functools.partial(_attn_kernel, tk=tk, rc=rc, coef=coef)
    return pl.pallas_call(
        kern,
        grid=(batch, N_HEADS, nq),
        in_specs=[pl.BlockSpec((tq, HEAD_PAD), lambda b, h, i: (b * nq + i, h)),
                  pl.BlockSpec((HEAD_PAD, seq), lambda b, h, i: (h, b)),
                  pl.BlockSpec((seq, V_HEAD), lambda b, h, i: (b, h)),
                  pl.BlockSpec((1, V_HEAD), lambda b, h, i: (0, h))],
        out_specs=pl.BlockSpec((tq, V_HEAD), lambda b, h, i: (b * nq + i, h)),
        out_shape=jax.ShapeDtypeStruct((n, N_HEADS * V_HEAD), BF16),
        scratch_shapes=[pltpu.VMEM((tq, LANES), F32), pltpu.VMEM((tq, LANES), F32),
                        pltpu.VMEM((tq, V_HEAD), F32)],
        compiler_params=_cparams(("arbitrary", "arbitrary", "arbitrary")),
        name="attention",
    )(q_pad, kt_pad, v, attn_out_norm)


def _out_proj_kernel(conv_ref, attn_ref, x_ref, wo_ref, fn_ref, wpqt_ref, keys_ref,
                     h1_ref, ft_ref, sc_ref):
    conv_ch = conv_ref.shape[1]
    mix = jnp.dot(conv_ref[...], wo_ref[0:conv_ch, :], preferred_element_type=F32)
    mix += jnp.dot(attn_ref[...], wo_ref[conv_ch:, :], preferred_element_type=F32)
    h1 = x_ref[...] + mix
    h1_ref[...] = h1
    ft = _rms(h1, fn_ref[...]).T.astype(BF16)
    ft_ref[...] = ft
    qpt = jnp.dot(wpqt_ref[...], ft, preferred_element_type=F32).astype(BF16)
    for hp in range(keys_ref.shape[0]):
        sc = jnp.dot(keys_ref[hp], qpt[hp * N_KEYS:(hp + 1) * N_KEYS, :],
                     preferred_element_type=F32)
        for c in range(sc_ref.shape[1]):
            sc_ref[hp, c] = sc[:, c * LANES:(c + 1) * LANES]


def _out_proj(conv_n, attn_n, x2, w_out, ffn_norm, w_pq_t, keys, *, tm):
    n, d = x2.shape
    nhp = keys.shape[0]
    tok = lambda w: pl.BlockSpec((tm, w), lambda i: (i, 0))
    return pl.pallas_call(
        _out_proj_kernel,
        grid=(n // tm,),
        in_specs=[tok(conv_n.shape[1]), tok(attn_n.shape[1]), tok(d), _resident(w_out.shape),
                  _resident((1, d)), _resident(w_pq_t.shape), _resident(keys.shape)],
        out_specs=[tok(d), pl.BlockSpec((d, tm), lambda i: (0, i)),
                   pl.BlockSpec((nhp, tm // LANES, N_KEYS, LANES), lambda i: (0, i, 0, 0))],
        out_shape=[jax.ShapeDtypeStruct((n, d), F32), jax.ShapeDtypeStruct((d, n), BF16),
                   jax.ShapeDtypeStruct((nhp, n // LANES, N_KEYS, LANES), F32)],
        compiler_params=_cparams(("arbitrary",)),
        name="out_proj",
    )(conv_n, attn_n, x2, w_out, ffn_norm, w_pq_t, keys)


def _sort_network(n):
    pairs = []

    def merge(lo, m, r):
        step = 2 * r
        if step < m:
            merge(lo, m, step)
            merge(lo + r, m, step)
            pairs.extend((i, i + r) for i in range(lo + r, lo + m - r, step))
        else:
            pairs.append((lo, lo + r))

    def sort(lo, m):
        if m > 1:
            sort(lo, m // 2)
            sort(lo + m // 2, m // 2)
            merge(lo, m, 1)

    sort(0, n)
    return pairs


def _top16(s):
    k = PEER_TOPK
    c = s.shape[1]
    g = [s[8 * j:8 * j + 8, :] for j in range(s.shape[0] // 8)]
    assert len(g) == k

    def exchange(i, j):
        g[i], g[j] = jnp.maximum(g[i], g[j]), jnp.minimum(g[i], g[j])

    for i, j in _sort_network(k):
        exchange(i, j)
    for shift in (4, 2, 1):
        other = [pltpu.roll(x, shift, 0) for x in g]
        g = [jnp.maximum(g[j], other[k - 1 - j]) for j in range(k)]
        dist = k // 2
        while dist:
            for i in range(k):
                if not i & dist:
                    exchange(i, i + dist)
            dist //= 2
    rows = lax.broadcasted_iota(jnp.int32, (k, c), 0)
    vals = jnp.zeros((k, c), F32)
    for j in range(k):
        vals = jnp.where(rows == j, jnp.concatenate([g[j], g[j]], axis=0), vals)
    return vals


def _peer_gate_kernel(sc_ref, e1_ref, t2_ref, e2_ref):
    n_chunks = sc_ref.shape[1]

    def one(h, c):
        s1 = sc_ref[2 * h, c]
        s2 = sc_ref[2 * h + 1, c]
        v1 = _top16(s1)
        v2 = _top16(s2)
        cand = jnp.concatenate([v1 + v2[0:1]] + [v1[0:8] + v2[b:b + 1] for b in range(1, 8)]
                               + [v1[0:1] + v2[8:16]], axis=0)
        best = v1[0:1] + v2[0:1]
        pad = jnp.full((N_KEYS - cand.shape[0], cand.shape[1]), -jnp.inf, F32)
        top = _top16(jnp.concatenate([cand, pad], axis=0))
        thr = top[PEER_TOPK - 1:PEER_TOPK]
        zsum = jnp.sum(jnp.exp(top - best), axis=0, keepdims=True)
        tsel = jnp.full_like(v1, jnp.inf)
        for b in range(PEER_TOPK):
            tsel = jnp.where(v1 + v2[b:b + 1] >= thr, v2[b:b + 1], tsel)
        t2 = jnp.full_like(s1, jnp.inf)
        for a in range(PEER_TOPK):
            t2 = jnp.where(s1 == v1[a:a + 1], tsel[a:a + 1], t2)
        e1_ref[h, c] = jnp.exp(s1 - v1[0:1])
        t2_ref[h, c] = t2
        e2_ref[h, c] = jnp.exp(s2 - v2[0:1]) * (0.5 / zsum)

    def body(t, carry):
        one(t // n_chunks, t % n_chunks)
        return carry

    lax.fori_loop(0, PEER_HEADS * n_chunks, body, 0)


def _peer_gate(scores, *, tm):
    nhp, n_chunks, nk, _ = scores.shape
    cpt = tm // LANES
    out = [jax.ShapeDtypeStruct((nhp // 2, n_chunks, nk, LANES), F32)] * 3
    spec = pl.BlockSpec((nhp // 2, cpt, nk, LANES), lambda i: (0, i, 0, 0))
    return pl.pallas_call(
        _peer_gate_kernel,
        grid=(n_chunks // cpt,),
        in_specs=[pl.BlockSpec((nhp, cpt, nk, LANES), lambda i: (0, i, 0, 0))],
        out_specs=[spec, spec, spec],
        out_shape=out,
        compiler_params=_cparams(("arbitrary",)),
        name="peer_gate",
    )(scores)


def _peer_dense_kernel(ft_ref, u_ref, vt_ref, e1_ref, t2_ref, e2_ref, s2_ref, o_ref,
                       h0_ref, h1_ref, w0_ref, w1_ref, *, n_blocks):
    s = pl.program_id(1)
    n_chunks, eb, _ = w0_ref.shape
    per_block = eb // N_KEYS

    @pl.when(s == 0)
    def _():
        o_ref[...] = jnp.zeros_like(o_ref)

    def stage(parity, hidden=True, gate=True, out=True):
        h_new, h_old = (h0_ref, h1_ref) if parity == 0 else (h1_ref, h0_ref)
        w_new, w_old = (w1_ref, w0_ref) if parity == 0 else (w0_ref, w1_ref)
        second = 1 - parity
        pair = (s - 1) // 2
        hm = eb // HIDDEN_ROW_PIECES

        def hidden_piece(m0, c0, k0, nk):
            res = jnp.dot(u_ref[m0:m0 + hm, k0:k0 + nk],
                          ft_ref[k0:k0 + nk, c0 * LANES:(c0 + 2) * LANES],
                          preferred_element_type=F32)
            if k0 == 0:
                h_new[c0, m0:m0 + hm, :] = res[:, :LANES]
                h_new[c0 + 1, m0:m0 + hm, :] = res[:, LANES:]
            else:
                h_new[c0, m0:m0 + hm, :] += res[:, :LANES]
                h_new[c0 + 1, m0:m0 + hm, :] += res[:, LANES:]

        def out_piece(r0, nr):
            w = jnp.concatenate([w_old[c] for c in range(n_chunks)], axis=1)
            o_ref[r0:r0 + nr, :] += jnp.dot(vt_ref[r0:r0 + nr, :], w, preferred_element_type=F32)

        def gate_tile(a, c):
            r = per_block * second + a
            rows = slice(a * N_KEYS, (a + 1) * N_KEYS)
            gate = jnp.zeros((N_KEYS, LANES), F32)
            for h in range(PEER_HEADS):
                t2 = t2_ref[h, c, pair, r:r + 1, :]
                e1 = e1_ref[h, c, pair, r:r + 1, :]
                gate += jnp.where(s2_ref[h, c] >= t2, e2_ref[h, c] * e1, 0.0)
            x = h_old[c, rows, :]
            gelu2 = x * (1.0 + lax.erf(x * (2.0 ** -0.5)))
            w_new[c, rows, :] = (gate * gelu2).astype(BF16)

        d = o_ref.shape[0]
        nk = d // HIDDEN_K_PIECES
        nr = d // OUT_ROW_PIECES
        pieces = []
        if hidden:
            pieces += [functools.partial(hidden_piece, m0, c0, k0, nk) for k0 in range(0, d, nk)
                       for m0 in range(0, eb, hm) for c0 in range(0, n_chunks, 2)]
        if out:
            pieces += [functools.partial(out_piece, r0, nr) for r0 in range(0, d, nr)]
        tiles = [(a, c) for a in range(per_block) for c in range(n_chunks)] if gate else []
        done = 0
        for q, piece in enumerate(pieces):
            piece()
            want = (q + 1) * len(tiles) // len(pieces)
            for a, c in tiles[done:want]:
                gate_tile(a, c)
            done = want

    pl.when(s == 0)(functools.partial(stage, 0, gate=False, out=False))
    pl.when(s == 1)(functools.partial(stage, 1, out=False))
    for parity in (0, 1):
        pl.when((s % 2 == parity) & (s >= 2) & (s < n_blocks))(functools.partial(stage, parity))
    pl.when(s == n_blocks)(functools.partial(stage, n_blocks % 2, hidden=False))
    pl.when(s == n_blocks + 1)(functools.partial(stage, (n_blocks + 1) % 2, hidden=False,
                                                 gate=False))


def _peer_dense(f_t, u_bf, v_t, e1, t2, e2, scores, *, t, eb):
    d, n = f_t.shape
    ne = u_bf.shape[0]
    assert ne % (2 * eb) == 0 and 2 * eb == 8 * N_KEYS
    assert t % (2 * LANES) == 0
    n_blocks = ne // eb
    cpt = t // LANES
    once = dict(pipeline_mode=pl.Buffered(1))
    gspec = pl.BlockSpec((PEER_HEADS, cpt, N_KEYS, LANES), lambda i, s: (0, i, 0, 0), **once)
    pspec = pl.BlockSpec((PEER_HEADS, cpt, N_KEYS // 8, 8, LANES), lambda i, s: (0, i, 0, 0, 0),
                         **once)
    s2spec = pl.BlockSpec((PEER_HEADS, None, cpt, N_KEYS, LANES), lambda i, s: (0, 1, i, 0, 0),
                          **once)
    e1 = e1.reshape(PEER_HEADS, n // LANES, N_KEYS // 8, 8, LANES)
    t2 = t2.reshape(PEER_HEADS, n // LANES, N_KEYS // 8, 8, LANES)
    scores = scores.reshape(PEER_HEADS, 2, n // LANES, N_KEYS, LANES)
    kern = functools.partial(_peer_dense_kernel, n_blocks=n_blocks)
    return pl.pallas_call(
        kern,
        grid=(n // t, n_blocks + 2),
        in_specs=[pl.BlockSpec((d, t), lambda i, s: (0, i), **once),
                  pl.BlockSpec((eb, d), lambda i, s: (jnp.minimum(s, n_blocks - 1), 0)),
                  pl.BlockSpec((None, d, eb), lambda i, s: (jnp.clip(s - 2, 0, n_blocks - 1), 0, 0)),
                  pspec, pspec, gspec, s2spec],
        out_specs=pl.BlockSpec((d, t), lambda i, s: (0, i)),
        out_shape=jax.ShapeDtypeStruct((d, n), F32),
        scratch_shapes=[pltpu.VMEM((cpt, eb, LANES), F32), pltpu.VMEM((cpt, eb, LANES), F32),
                        pltpu.VMEM((cpt, eb, LANES), BF16), pltpu.VMEM((cpt, eb, LANES), BF16)],
        compiler_params=_cparams(("arbitrary", "arbitrary")),
        name="peer_dense",
    )(f_t, u_bf, v_t, e1, t2, e2, scores)


def _ple_kernel(h1_ref, peer_ref, p_ref, pn_ref, wg_ref, wp_ref, fin_ref, o_ref, *, last_layer):
    h2 = h1_ref[...] + peer_ref[...].T
    gate = jax.nn.sigmoid(jnp.dot(_rms(h2, pn_ref[...]).astype(BF16), wg_ref[...],
                                  preferred_element_type=F32))
    proj = jnp.dot(p_ref[...].astype(BF16), wp_ref[...], preferred_element_type=F32)
    h3 = h2 + proj * gate
    o_ref[...] = _rms(h3, fin_ref[...]) if last_layer else h3


def _ple_final(h1, peer, p2, ple_norm, w_gate, w_proj, final_norm, *, tm, last_layer):
    n, d = h1.shape
    tok = lambda w: pl.BlockSpec((tm, w), lambda i: (i, 0))
    return pl.pallas_call(
        functools.partial(_ple_kernel, last_layer=last_layer),
        grid=(n // tm,),
        in_specs=[tok(d), pl.BlockSpec((d, tm), lambda i: (0, i)), tok(p2.shape[1]),
                  _resident((1, d)), _resident(w_gate.shape),
                  _resident(w_proj.shape), _resident((1, d))],
        out_specs=tok(d),
        out_shape=jax.ShapeDtypeStruct((n, d), F32),
        compiler_params=_cparams(("arbitrary",)),
        name="ple_final",
    )(h1, peer, p2, ple_norm, w_gate, w_proj, final_norm)


def _swap_halves(w):
    half = w.shape[-1] // 2
    return jnp.concatenate([w[..., half:], w[..., :half]], axis=-1)


def _tile(n, pref):
    t = min(n, pref)
    assert n % t == 0
    return t


def _tiles(n, seq):
    return dict(
        in_proj=_tile(seq, 512),
        attn_q=_tile(seq, 2048), attn_rows=_tile(seq, 1024), attn_k=_tile(seq, 1024),
        out_proj=_tile(n, 512), peer_gate=_tile(n, 512), ple=_tile(n, 512),
        peer_tokens=_tile(n, 1024),
        peer_experts=4 * N_KEYS)


def kernel(x, p, positions, attn_norm, w_in, conv_w, q_norm, w_uq, kv_norm, w_ukv, conv_out_norm,
           attn_out_norm, w_out, ffn_norm, w_pq, sub_keys, u_tab, v_tab, ple_norm, w_ple_gate,
           w_ple_proj, final_norm):
    batch, seq, d = x.shape
    n = batch * seq
    depth = w_in.shape[0]
    tiles = _tiles(n, seq)
    row = lambda g: g.reshape(1, -1)

    inv_freq = ROPE_THETA ** (-(jnp.arange(0, QK_ROPE, 2, dtype=F32) / QK_ROPE))
    zeros = jnp.zeros((LANES - QK_ROPE,), F32)
    freq_row = row(jnp.concatenate([inv_freq, inv_freq, zeros]))
    sign_row = row(jnp.concatenate([-jnp.ones_like(inv_freq), jnp.ones_like(inv_freq), zeros]))
    pos2 = positions.reshape(n, 1)

    h = x.reshape(n, d)
    for i in range(depth):
        kr_cols = w_in[i][:, -QK_ROPE:]
        w_in_ext = jnp.concatenate([w_in[i], _swap_halves(kr_cols)], axis=-1).astype(BF16)
        uq = w_uq[i].reshape(-1, N_HEADS, QK_NOPE + QK_ROPE)
        uq_rope = uq[..., QK_NOPE:]
        w_uq_ext = jnp.concatenate([uq, _swap_halves(uq_rope)], axis=-1)
        w_uq_ext = w_uq_ext.reshape(-1, N_HEADS * HEAD_PAD).astype(BF16)
        ukv = w_ukv[i].reshape(-1, N_HEADS, QK_NOPE + V_HEAD)
        w_ukv_p = jnp.concatenate([ukv[..., :QK_NOPE].reshape(-1, N_HEADS * QK_NOPE),
                                   ukv[..., QK_NOPE:].reshape(-1, N_HEADS * V_HEAD)],
                                  axis=-1).astype(BF16)
        keys = sub_keys[i].reshape(-1, N_KEYS, sub_keys.shape[-1]).astype(BF16)

        conv_n, q_pad, k_pad, v = _in_proj(
            h, pos2, row(attn_norm[i]), w_in_ext, conv_w[i], row(q_norm[i]), w_uq_ext,
            row(kv_norm[i]), w_ukv_p, row(conv_out_norm[i]), freq_row, sign_row,
            seq=seq, tm=tiles["in_proj"])
        attn_n = _attention(q_pad, k_pad, v, row(attn_out_norm[i]), batch=batch, seq=seq,
                            tq=tiles["attn_q"], tk=tiles["attn_k"], rc=tiles["attn_rows"])
        h1, f_t, scores = _out_proj(conv_n, attn_n, h, w_out[i].astype(BF16), row(ffn_norm[i]),
                                    w_pq[i].T.astype(BF16), keys, tm=tiles["out_proj"])
        e1, t2, e2 = _peer_gate(scores, tm=tiles["peer_gate"])
        eb = tiles["peer_experts"]
        v_blocks = v_tab[i].reshape(-1, eb, d).transpose(0, 2, 1).astype(BF16)
        peer = _peer_dense(f_t, u_tab[i].astype(BF16), v_blocks, e1, t2, e2, scores,
                           t=tiles["peer_tokens"], eb=eb)
        h = _ple_final(h1, peer, p[i].reshape(n, -1), row(ple_norm[i]),
                       w_ple_gate[i].astype(BF16), w_ple_proj[i].astype(BF16),
                       row(final_norm), tm=tiles["ple"], last_layer=i == depth - 1)
    return h.reshape(batch, seq, d)
```

```python
import functools

import jax
import jax.numpy as jnp
from jax import lax
from jax.experimental import pallas as pl
from jax.experimental.pallas import tpu as pltpu

EPS = 1e-6
ROPE_THETA = 10000.0
CONV_GROUP_DIM = 128
N_HEADS = 8
QK_NOPE = 128
QK_ROPE = 64
V_HEAD = 128
HEAD_PAD = 256
PEER_HEADS = 8
PEER_TOPK = 16
N_KEYS = 128
LANES = 128
HIDDEN_K_PIECES = 4
HIDDEN_ROW_PIECES = 1
OUT_ROW_PIECES = 16
NEG_BIG = -1e30

VMEM_LIMIT = 56 * 1024 * 1024

F32 = jnp.float32
BF16 = jnp.bfloat16


def _cparams(semantics):
    return pltpu.CompilerParams(dimension_semantics=semantics, vmem_limit_bytes=VMEM_LIMIT)


def _resident(shape):
    nd = len(shape)
    return pl.BlockSpec(shape, lambda *_: (0,) * nd, pipeline_mode=pl.Buffered(1))


def _rms(xf, g):
    return xf * lax.rsqrt(jnp.mean(xf * xf, axis=-1, keepdims=True) + EPS) * g


def _in_proj_kernel(x_ref, pos_ref, an_ref, win_ref, cw_ref, qn_ref, wuq_ref, kvn_ref, wukv_ref,
                    con_ref, freq_ref, sign_ref,
                    conv_ref, q_ref, kt_ref, v_ref, carry_ref, *, tiles_per_seq, conv_ch, q_lora,
                    kv_lora):
    i = pl.program_id(0)
    tm = x_ref.shape[0]

    @pl.when(i % tiles_per_seq == 0)
    def _():
        carry_ref[...] = jnp.zeros_like(carry_ref)

    a = _rms(x_ref[...], an_ref[...]).astype(BF16)

    xin = jnp.dot(a, win_ref[:, 0:conv_ch], preferred_element_type=F32)
    c_g = jnp.dot(a, win_ref[:, 2 * conv_ch:3 * conv_ch], preferred_element_type=F32)
    u = c_g * xin
    prev = carry_ref[...]
    row = lax.broadcasted_iota(jnp.int32, u.shape, 0)
    u1 = jnp.where(row == 0, prev[7:8], pltpu.roll(u, 1, 0))
    u2 = jnp.where(row == 0, prev[6:7], jnp.where(row == 1, prev[7:8], pltpu.roll(u, 2, 0)))
    carry_ref[...] = u[tm - 8:tm]
    y = cw_ref[2:3] * u + cw_ref[1:2] * u1 + cw_ref[0:1] * u2
    b_g = jnp.dot(a, win_ref[:, conv_ch:2 * conv_ch], preferred_element_type=F32)
    conv_o = b_g * y
    for g in range(conv_ch // CONV_GROUP_DIM):
        sl = slice(g * CONV_GROUP_DIM, (g + 1) * CONV_GROUP_DIM)
        conv_ref[:, sl] = _rms(conv_o[:, sl], con_ref[:, sl]).astype(BF16)

    m0 = 3 * conv_ch
    lat = jnp.dot(a, win_ref[:, m0:], preferred_element_type=F32)
    ang = pos_ref[...].astype(F32) * freq_ref[...]
    cos_t = jnp.cos(ang) * jnp.abs(sign_ref[...])
    sin_t = jnp.sin(ang) * sign_ref[...]

    def rope(blk):
        return blk * cos_t + pltpu.roll(blk, QK_ROPE, 1) * sin_t

    cq = _rms(lat[:, 0:q_lora], qn_ref[...]).astype(BF16)
    qf = jnp.dot(cq, wuq_ref[...], preferred_element_type=F32)
    ckv = _rms(lat[:, q_lora:q_lora + kv_lora], kvn_ref[...]).astype(BF16)
    kvf = jnp.dot(ckv, wukv_ref[...], preferred_element_type=F32)
    kr_t = rope(lat[:, q_lora + kv_lora:]).T.astype(BF16)
    for h in range(N_HEADS):
        o = h * HEAD_PAD
        q_ref[:, o:o + QK_NOPE] = qf[:, o:o + QK_NOPE].astype(BF16)
        q_ref[:, o + QK_NOPE:o + HEAD_PAD] = rope(qf[:, o + QK_NOPE:o + HEAD_PAD]).astype(BF16)
        kt_ref[o:o + QK_NOPE, :] = kvf[:, h * QK_NOPE:(h + 1) * QK_NOPE].T.astype(BF16)
        kt_ref[o + QK_NOPE:o + HEAD_PAD, :] = kr_t
    v_ref[...] = kvf[:, N_HEADS * QK_NOPE:].astype(BF16)


def _in_proj(x2, pos2, attn_norm, w_in_ext, conv_w, q_norm, w_uq_ext, kv_norm, w_ukv_p,
             conv_out_norm, freq_row, sign_row, *, seq, tm):
    n, d = x2.shape
    conv_ch = conv_w.shape[-1]
    q_lora, kv_lora = q_norm.shape[-1], kv_norm.shape[-1]
    assert seq % tm == 0 and n % seq == 0
    tok = lambda w: pl.BlockSpec((tm, w), lambda i: (i, 0))
    kern = functools.partial(_in_proj_kernel, tiles_per_seq=seq // tm, conv_ch=conv_ch,
                             q_lora=q_lora, kv_lora=kv_lora)
    return pl.pallas_call(
        kern,
        grid=(n // tm,),
        in_specs=[tok(d), tok(1), _resident((1, d)), _resident(w_in_ext.shape),
                  _resident(conv_w.shape), _resident((1, q_lora)), _resident(w_uq_ext.shape),
                  _resident((1, kv_lora)), _resident(w_ukv_p.shape), _resident((1, conv_ch)),
                  _resident((1, LANES)), _resident((1, LANES))],
        out_specs=[tok(conv_ch), tok(N_HEADS * HEAD_PAD),
                   pl.BlockSpec((N_HEADS * HEAD_PAD, tm), lambda i: (0, i)),
                   tok(N_HEADS * V_HEAD)],
        out_shape=[jax.ShapeDtypeStruct((n, conv_ch), BF16),
                   jax.ShapeDtypeStruct((n, N_HEADS * HEAD_PAD), BF16),
                   jax.ShapeDtypeStruct((N_HEADS * HEAD_PAD, n), BF16),
                   jax.ShapeDtypeStruct((n, N_HEADS * V_HEAD), BF16)],
        scratch_shapes=[pltpu.VMEM((8, conv_ch), F32)],
        compiler_params=_cparams(("arbitrary",)),
        name="in_proj",
    )(x2, pos2, attn_norm, w_in_ext, conv_w, q_norm, w_uq_ext, kv_norm, w_ukv_p, conv_out_norm,
      freq_row, sign_row)


def _attn_kernel(q_ref, kt_ref, v_ref, g_ref, o_ref, m_ref, l_ref, acc_ref, *, tk, rc, coef):
    qi = pl.program_id(2)
    tq = q_ref.shape[0]
    n_diag = tq // tk
    m_ref[...] = jnp.full_like(m_ref, NEG_BIG)
    l_ref[...] = jnp.zeros_like(l_ref)
    acc_ref[...] = jnp.zeros_like(acc_ref)

    def chunk_step(kb, r0, mask_offset):
        rows = slice(r0, r0 + rc)
        start = pl.multiple_of(kb * tk, tk)
        kt = kt_ref[:, pl.ds(start, tk)]
        v = v_ref[pl.ds(start, tk), :]
        s = jnp.dot(q_ref[rows, :], kt, preferred_element_type=F32)
        slabs = [s[:, j * LANES:(j + 1) * LANES] for j in range(tk // LANES)]
        if mask_offset is not None:
            r = lax.broadcasted_iota(jnp.int32, (rc, LANES), 0) + r0
            c = lax.broadcasted_iota(jnp.int32, (rc, LANES), 1)
            slabs = [jnp.where(c + (mask_offset + j * LANES) <= r, sl, NEG_BIG)
                     for j, sl in enumerate(slabs)]
        m_old = m_ref[rows, :]
        m_new = jnp.maximum(m_old, jnp.max(functools.reduce(jnp.maximum, slabs), axis=-1,
                                           keepdims=True))
        alpha = jnp.exp2((m_old - m_new) * coef)
        ps = [jnp.exp2((sl - m_new) * coef) for sl in slabs]
        l_ref[rows, :] = alpha * l_ref[rows, :] + jnp.sum(functools.reduce(jnp.add, ps), axis=-1,
                                                          keepdims=True)
        p = jnp.concatenate([x.astype(BF16) for x in ps], axis=1)
        acc_ref[rows, :] = alpha * acc_ref[rows, :] + jnp.dot(p, v, preferred_element_type=F32)
        m_ref[rows, :] = m_new

    n_full = qi * n_diag

    def body(kb, carry):
        for r0 in range(0, tq, rc):
            chunk_step(kb, r0, None)
        return carry

    lax.fori_loop(0, n_full, body, 0)
    for j in range(n_diag):
        for r0 in range(0, tq, rc):
            if r0 + rc - 1 < j * tk:
                continue
            needs_mask = r0 < j * tk + tk - 1
            chunk_step(n_full + j, r0, j * tk if needs_mask else None)

    o = acc_ref[...] / l_ref[...]
    o_ref[...] = _rms(o, g_ref[...]).astype(BF16)


def _attention(q_pad, kt_pad, v, attn_out_norm, *, batch, seq, tq, tk, rc):
    n = q_pad.shape[0]
    assert seq % tq == 0 and tq % tk == 0 and tq % rc == 0 and tk % LANES == 0
    nq = seq // tq
    coef = float(QK_NOPE + QK_ROPE) ** -0.5 * 1.4426950408889634
    kern = functools.partial(_attn_kernel, tk=tk, rc=rc, coef=coef)
    return pl.pallas_call(
        kern,
        grid=(batch, N_HEADS, nq),
        in_specs=[pl.BlockSpec((tq, HEAD_PAD), lambda b, h, i: (b * nq + i, h)),
                  pl.BlockSpec((HEAD_PAD, seq), lambda b, h, i: (h, b)),
                  pl.BlockSpec((seq, V_HEAD), lambda b, h, i: (b, h)),
                  pl.BlockSpec((1, V_HEAD), lambda b, h, i: (0, h))],
        out_specs=pl.BlockSpec((tq, V_HEAD), lambda b, h, i: (b * nq + i, h)),
        out_shape=jax.ShapeDtypeStruct((n, N_HEADS * V_HEAD), BF16),
        scratch_shapes=[pltpu.VMEM((tq, LANES), F32), pltpu.VMEM((tq, LANES), F32),
                        pltpu.VMEM((tq, V_HEAD), F32)],
        compiler_params=_cparams(("arbitrary", "arbitrary", "arbitrary")),
        name="attention",
    )(q_pad, kt_pad, v, attn_out_norm)


def _out_proj_kernel(conv_ref, attn_ref, x_ref, wo_ref, fn_ref, wpqt_ref, keys_ref,
                     h1_ref, ft_ref, sc_ref):
    conv_ch = conv_ref.shape[1]
    mix = jnp.dot(conv_ref[...], wo_ref[0:conv_ch, :], preferred_element_type=F32)
    mix += jnp.dot(attn_ref[...], wo_ref[conv_ch:, :], preferred_element_type=F32)
    h1 = x_ref[...] + mix
    h1_ref[...] = h1
    ft = _rms(h1, fn_ref[...]).T.astype(BF16)
    ft_ref[...] = ft
    qpt = jnp.dot(wpqt_ref[...], ft, preferred_element_type=F32).astype(BF16)
    for hp in range(keys_ref.shape[0]):
        sc = jnp.dot(keys_ref[hp], qpt[hp * N_KEYS:(hp + 1) * N_KEYS, :],
                     preferred_element_type=F32)
        for c in range(sc_ref.shape[1]):
            sc_ref[hp, c] = sc[:, c * LANES:(c + 1) * LANES]


def _out_proj(conv_n, attn_n, x2, w_out, ffn_norm, w_pq_t, keys, *, tm):
    n, d = x2.shape
    nhp = keys.shape[0]
    tok = lambda w: pl.BlockSpec((tm, w), lambda i: (i, 0))
    return pl.pallas_call(
        _out_proj_kernel,
        grid=(n // tm,),
        in_specs=[tok(conv_n.shape[1]), tok(attn_n.shape[1]), tok(d), _resident(w_out.shape),
                  _resident((1, d)), _resident(w_pq_t.shape), _resident(keys.shape)],
        out_specs=[tok(d), pl.BlockSpec((d, tm), lambda i: (0, i)),
                   pl.BlockSpec((nhp, tm // LANES, N_KEYS, LANES), lambda i: (0, i, 0, 0))],
        out_shape=[jax.ShapeDtypeStruct((n, d), F32), jax.ShapeDtypeStruct((d, n), BF16),
                   jax.ShapeDtypeStruct((nhp, n // LANES, N_KEYS, LANES), F32)],
        compiler_params=_cparams(("arbitrary",)),
        name="out_proj",
    )(conv_n, attn_n, x2, w_out, ffn_norm, w_pq_t, keys)


def _sort_network(n):
    pairs = []

    def merge(lo, m, r):
        step = 2 * r
        if step < m:
            merge(lo, m, step)
            merge(lo + r, m, step)
            pairs.extend((i, i + r) for i in range(lo + r, lo + m - r, step))
        else:
            pairs.append((lo, lo + r))

    def sort(lo, m):
        if m > 1:
            sort(lo, m // 2)
            sort(lo + m // 2, m // 2)
            merge(lo, m, 1)

    sort(0, n)
    return pairs


def _top16(s):
    k = PEER_TOPK
    c = s.shape[1]
    g = [s[8 * j:8 * j + 8, :] for j in range(s.shape[0] // 8)]
    assert len(g) == k

    def exchange(i, j):
        g[i], g[j] = jnp.maximum(g[i], g[j]), jnp.minimum(g[i], g[j])

    for i, j in _sort_network(k):
        exchange(i, j)
    for shift in (4, 2, 1):
        other = [pltpu.roll(x, shift, 0) for x in g]
        g = [jnp.maximum(g[j], other[k - 1 - j]) for j in range(k)]
        dist = k // 2
        while dist:
            for i in range(k):
                if not i & dist:
                    exchange(i, i + dist)
            dist //= 2
    rows = lax.broadcasted_iota(jnp.int32, (k, c), 0)
    vals = jnp.zeros((k, c), F32)
    for j in range(k):
        vals = jnp.where(rows == j, jnp.concatenate([g[j], g[j]], axis=0), vals)
    return vals


def _peer_gate_kernel(sc_ref, e1_ref, t2_ref, e2_ref):
    n_chunks = sc_ref.shape[1]

    def one(h, c):
        s1 = sc_ref[2 * h, c]
        s2 = sc_ref[2 * h + 1, c]
        v1 = _top16(s1)
        v2 = _top16(s2)
        cand = jnp.concatenate([v1 + v2[0:1]] + [v1[0:8] + v2[b:b + 1] for b in range(1, 8)]
                               + [v1[0:1] + v2[8:16]], axis=0)
        best = v1[0:1] + v2[0:1]
        pad = jnp.full((N_KEYS - cand.shape[0], cand.shape[1]), -jnp.inf, F32)
        top = _top16(jnp.concatenate([cand, pad], axis=0))
        thr = top[PEER_TOPK - 1:PEER_TOPK]
        zsum = jnp.sum(jnp.exp(top - best), axis=0, keepdims=True)
        tsel = jnp.full_like(v1, jnp.inf)
        for b in range(PEER_TOPK):
            tsel = jnp.where(v1 + v2[b:b + 1] >= thr, v2[b:b + 1], tsel)
        t2 = jnp.full_like(s1, jnp.inf)
        for a in range(PEER_TOPK):
            t2 = jnp.where(s1 == v1[a:a + 1], tsel[a:a + 1], t2)
        e1_ref[h, c] = jnp.exp(s1 - v1[0:1])
        t2_ref[h, c] = t2
        e2_ref[h, c] = jnp.exp(s2 - v2[0:1]) * (0.5 / zsum)

    def body(t, carry):
        one(t // n_chunks, t % n_chunks)
        return carry

    lax.fori_loop(0, PEER_HEADS * n_chunks, body, 0)


def _peer_gate(scores, *, tm):
    nhp, n_chunks, nk, _ = scores.shape
    cpt = tm // LANES
    out = [jax.ShapeDtypeStruct((nhp // 2, n_chunks, nk, LANES), F32)] * 3
    spec = pl.BlockSpec((nhp // 2, cpt, nk, LANES), lambda i: (0, i, 0, 0))
    return pl.pallas_call(
        _peer_gate_kernel,
        grid=(n_chunks // cpt,),
        in_specs=[pl.BlockSpec((nhp, cpt, nk, LANES), lambda i: (0, i, 0, 0))],
        out_specs=[spec, spec, spec],
        out_shape=out,
        compiler_params=_cparams(("arbitrary",)),
        name="peer_gate",
    )(scores)


def _peer_dense_kernel(ft_ref, u_ref, vt_ref, e1_ref, t2_ref, e2_ref, s2_ref, o_ref,
                       h0_ref, h1_ref, w0_ref, w1_ref, *, n_blocks):
    s = pl.program_id(1)
    n_chunks, eb, _ = w0_ref.shape
    per_block = eb // N_KEYS

    @pl.when(s == 0)
    def _():
        o_ref[...] = jnp.zeros_like(o_ref)

    def stage(parity, hidden=True, gate=True, out=True):
        h_new, h_old = (h0_ref, h1_ref) if parity == 0 else (h1_ref, h0_ref)
        w_new, w_old = (w1_ref, w0_ref) if parity == 0 else (w0_ref, w1_ref)
        second = 1 - parity
        pair = (s - 1) // 2
        hm = eb // HIDDEN_ROW_PIECES

        def hidden_piece(m0, c0, k0, nk):
            res = jnp.dot(u_ref[m0:m0 + hm, k0:k0 + nk],
                          ft_ref[k0:k0 + nk, c0 * LANES:(c0 + 2) * LANES],
                          preferred_element_type=F32)
            if k0 == 0:
                h_new[c0, m0:m0 + hm, :] = res[:, :LANES]
                h_new[c0 + 1, m0:m0 + hm, :] = res[:, LANES:]
            else:
                h_new[c0, m0:m0 + hm, :] += res[:, :LANES]
                h_new[c0 + 1, m0:m0 + hm, :] += res[:, LANES:]

        def out_piece(r0, nr):
            w = jnp.concatenate([w_old[c] for c in range(n_chunks)], axis=1)
            o_ref[r0:r0 + nr, :] += jnp.dot(vt_ref[r0:r0 + nr, :], w, preferred_element_type=F32)

        def gate_tile(a, c):
            r = per_block * second + a
            rows = slice(a * N_KEYS, (a + 1) * N_KEYS)
            gate = None
            for h in range(PEER_HEADS):
                t2 = t2_ref[h, c, pair, r:r + 1, :]
                e1 = e1_ref[h, c, pair, r:r + 1, :]
                term = jnp.where(s2_ref[h, c] >= t2, e2_ref[h, c] * e1, 0.0)
                gate = term if gate is None else gate + term
            x = h_old[c, rows, :]
            gelu2 = x * (1.0 + lax.erf(x * (2.0 ** -0.5)))
            w_new[c, rows, :] = (gate * gelu2).astype(BF16)

        d = o_ref.shape[0]
        nk = d // HIDDEN_K_PIECES
        nr = d // OUT_ROW_PIECES
        pieces = []
        if hidden:
            pieces += [functools.partial(hidden_piece, m0, c0, k0, nk) for k0 in range(0, d, nk)
                       for m0 in range(0, eb, hm) for c0 in range(0, n_chunks, 2)]
        if out:
            pieces += [functools.partial(out_piece, r0, nr) for r0 in range(0, d, nr)]
        tiles = [(a, c) for a in range(per_block) for c in range(n_chunks)] if gate else []
        done = 0
        for q, piece in enumerate(pieces):
            piece()
            want = (q + 1) * len(tiles) // len(pieces)
            for a, c in tiles[done:want]:
                gate_tile(a, c)
            done = want

    pl.when(s == 0)(functools.partial(stage, 0, gate=False, out=False))
    pl.when(s == 1)(functools.partial(stage, 1, out=False))
    for parity in (0, 1):
        pl.when((s % 2 == parity) & (s >= 2) & (s < n_blocks))(functools.partial(stage, parity))
    pl.when(s == n_blocks)(functools.partial(stage, n_blocks % 2, hidden=False))
    pl.when(s == n_blocks + 1)(functools.partial(stage, (n_blocks + 1) % 2, hidden=False,
                                                 gate=False))


def _peer_dense(f_t, u_bf, v_t, e1, t2, e2, scores, *, t, eb):
    d, n = f_t.shape
    ne = u_bf.shape[0]
    assert ne % (2 * eb) == 0 and 2 * eb == 8 * N_KEYS
    assert t % (2 * LANES) == 0
    n_blocks = ne // eb
    cpt = t // LANES
    once = dict(pipeline_mode=pl.Buffered(1))
    gspec = pl.BlockSpec((PEER_HEADS, cpt, N_KEYS, LANES), lambda i, s: (0, i, 0, 0), **once)
    pspec = pl.BlockSpec((PEER_HEADS, cpt, N_KEYS // 8, 8, LANES), lambda i, s: (0, i, 0, 0, 0),
                         **once)
    s2spec = pl.BlockSpec((PEER_HEADS, None, cpt, N_KEYS, LANES), lambda i, s: (0, 1, i, 0, 0),
                          **once)
    e1 = e1.reshape(PEER_HEADS, n // LANES, N_KEYS // 8, 8, LANES)
    t2 = t2.reshape(PEER_HEADS, n // LANES, N_KEYS // 8, 8, LANES)
    scores = scores.reshape(PEER_HEADS, 2, n // LANES, N_KEYS, LANES)
    kern = functools.partial(_peer_dense_kernel, n_blocks=n_blocks)
    return pl.pallas_call(
        kern,
        grid=(n // t, n_blocks + 2),
        in_specs=[pl.BlockSpec((d, t), lambda i, s: (0, i), **once),
                  pl.BlockSpec((eb, d), lambda i, s: (jnp.minimum(s, n_blocks - 1), 0)),
                  pl.BlockSpec((None, d, eb), lambda i, s: (jnp.clip(s - 2, 0, n_blocks - 1), 0, 0)),
                  pspec, pspec, gspec, s2spec],
        out_specs=pl.BlockSpec((d, t), lambda i, s: (0, i)),
        out_shape=jax.ShapeDtypeStruct((d, n), F32),
        scratch_shapes=[pltpu.VMEM((cpt, eb, LANES), F32), pltpu.VMEM((cpt, eb, LANES), F32),
                        pltpu.VMEM((cpt, eb, LANES), BF16), pltpu.VMEM((cpt, eb, LANES), BF16)],
        compiler_params=_cparams(("arbitrary", "arbitrary")),
        name="peer_dense",
    )(f_t, u_bf, v_t, e1, t2, e2, scores)


def _ple_kernel(h1_ref, peer_ref, p_ref, pn_ref, wg_ref, wp_ref, fin_ref, o_ref, *, last_layer):
    h2 = h1_ref[...] + peer_ref[...].T
    gate = jax.nn.sigmoid(jnp.dot(_rms(h2, pn_ref[...]).astype(BF16), wg_ref[...],
                                  preferred_element_type=F32))
    proj = jnp.dot(p_ref[...].astype(BF16), wp_ref[...], preferred_element_type=F32)
    h3 = h2 + proj * gate
    o_ref[...] = _rms(h3, fin_ref[...]) if last_layer else h3


def _ple_final(h1, peer, p2, ple_norm, w_gate, w_proj, final_norm, *, tm, last_layer):
    n, d = h1.shape
    tok = lambda w: pl.BlockSpec((tm, w), lambda i: (i, 0))
    return pl.pallas_call(
        functools.partial(_ple_kernel, last_layer=last_layer),
        grid=(n // tm,),
        in_specs=[tok(d), pl.BlockSpec((d, tm), lambda i: (0, i)), tok(p2.shape[1]),
                  _resident((1, d)), _resident(w_gate.shape),
                  _resident(w_proj.shape), _resident((1, d))],
        out_specs=tok(d),
        out_shape=jax.ShapeDtypeStruct((n, d), F32),
        compiler_params=_cparams(("arbitrary",)),
        name="ple_final",
    )(h1, peer, p2, ple_norm, w_gate, w_proj, final_norm)


def _swap_halves(w):
    half = w.shape[-1] // 2
    return jnp.concatenate([w[..., half:], w[..., :half]], axis=-1)


def _tile(n, pref):
    t = min(n, pref)
    assert n % t == 0
    return t


def _tiles(n, seq):
    return dict(
        in_proj=_tile(seq, 512),
        attn_q=_tile(seq, 2048), attn_rows=_tile(seq, 1024), attn_k=_tile(seq, 1024),
        out_proj=_tile(n, 512), peer_gate=_tile(n, 512), ple=_tile(n, 512),
        peer_tokens=_tile(n, 1024),
        peer_experts=4 * N_KEYS)


def kernel(x, p, positions, attn_norm, w_in, conv_w, q_norm, w_uq, kv_norm, w_ukv, conv_out_norm,
           attn_out_norm, w_out, ffn_norm, w_pq, sub_keys, u_tab, v_tab, ple_norm, w_ple_gate,
           w_ple_proj, final_norm):
    batch, seq, d = x.shape
    n = batch * seq
    depth = w_in.shape[0]
    tiles = _tiles(n, seq)
    row = lambda g: g.reshape(1, -1)

    inv_freq = ROPE_THETA ** (-(jnp.arange(0, QK_ROPE, 2, dtype=F32) / QK_ROPE))
    zeros = jnp.zeros((LANES - QK_ROPE,), F32)
    freq_row = row(jnp.concatenate([inv_freq, inv_freq, zeros]))
    sign_row = row(jnp.concatenate([-jnp.ones_like(inv_freq), jnp.ones_like(inv_freq), zeros]))
    pos2 = positions.reshape(n, 1)

    h = x.reshape(n, d)
    for i in range(depth):
        kr_cols = w_in[i][:, -QK_ROPE:]
        w_in_ext = jnp.concatenate([w_in[i], _swap_halves(kr_cols)], axis=-1).astype(BF16)
        uq = w_uq[i].reshape(-1, N_HEADS, QK_NOPE + QK_ROPE)
        uq_rope = uq[..., QK_NOPE:]
        w_uq_ext = jnp.concatenate([uq, _swap_halves(uq_rope)], axis=-1)
        w_uq_ext = w_uq_ext.reshape(-1, N_HEADS * HEAD_PAD).astype(BF16)
        ukv = w_ukv[i].reshape(-1, N_HEADS, QK_NOPE + V_HEAD)
        w_ukv_p = jnp.concatenate([ukv[..., :QK_NOPE].reshape(-1, N_HEADS * QK_NOPE),
                                   ukv[..., QK_NOPE:].reshape(-1, N_HEADS * V_HEAD)],
                                  axis=-1).astype(BF16)
        keys = sub_keys[i].reshape(-1, N_KEYS, sub_keys.shape[-1]).astype(BF16)

        conv_n, q_pad, k_pad, v = _in_proj(
            h, pos2, row(attn_norm[i]), w_in_ext, conv_w[i], row(q_norm[i]), w_uq_ext,
            row(kv_norm[i]), w_ukv_p, row(conv_out_norm[i]), freq_row, sign_row,
            seq=seq, tm=tiles["in_proj"])
        attn_n = _attention(q_pad, k_pad, v, row(attn_out_norm[i]), batch=batch, seq=seq,
                            tq=tiles["attn_q"], tk=tiles["attn_k"], rc=tiles["attn_rows"])
        h1, f_t, scores = _out_proj(conv_n, attn_n, h, w_out[i].astype(BF16), row(ffn_norm[i]),
                                    w_pq[i].T.astype(BF16), keys, tm=tiles["out_proj"])
        e1, t2, e2 = _peer_gate(scores, tm=tiles["peer_gate"])
        eb = tiles["peer_experts"]
        v_blocks = v_tab[i].reshape(-1, eb, d).transpose(0, 2, 1).astype(BF16)
        peer = _peer_dense(f_t, u_tab[i].astype(BF16), v_blocks, e1, t2, e2, scores,
                           t=tiles["peer_tokens"], eb=eb)
        h = _ple_final(h1, peer, p[i].reshape(n, -1), row(ple_norm[i]),
                       w_ple_gate[i].astype(BF16), w_ple_proj[i].astype(BF16),
                       row(final_norm), tm=tiles["ple"], last_layer=i == depth - 1)
    return h.reshape(batch, seq, d)
```

```python
import functools

import jax
import jax.numpy as jnp
from jax import lax
from jax.experimental import pallas as pl
from jax.experimental.pallas import tpu as pltpu

EPS = 1e-6
ROPE_THETA = 10000.0
CONV_GROUP_DIM = 128
N_HEADS = 8
QK_NOPE = 128
QK_ROPE = 64
V_HEAD = 128
HEAD_PAD = 256
PEER_HEADS = 8
PEER_TOPK = 16
N_KEYS = 128
LANES = 128
HIDDEN_K_PIECES = 4
HIDDEN_ROW_PIECES = 1
OUT_ROW_PIECES = 16
GATE_TILE_ROWS = 64
NEG_BIG = -1e30

VMEM_LIMIT = 56 * 1024 * 1024

F32 = jnp.float32
BF16 = jnp.bfloat16


def _cparams(semantics):
    return pltpu.CompilerParams(dimension_semantics=semantics, vmem_limit_bytes=VMEM_LIMIT)


def _resident(shape):
    nd = len(shape)
    return pl.BlockSpec(shape, lambda *_: (0,) * nd, pipeline_mode=pl.Buffered(1))


def _rms(xf, g):
    return xf * lax.rsqrt(jnp.mean(xf * xf, axis=-1, keepdims=True) + EPS) * g


def _in_proj_kernel(x_ref, pos_ref, an_ref, win_ref, cw_ref, qn_ref, wuq_ref, kvn_ref, wukv_ref,
                    con_ref, freq_ref, sign_ref,
                    conv_ref, q_ref, kt_ref, v_ref, carry_ref, *, tiles_per_seq, conv_ch, q_lora,
                    kv_lora):
    i = pl.program_id(0)
    tm = x_ref.shape[0]

    @pl.when(i % tiles_per_seq == 0)
    def _():
        carry_ref[...] = jnp.zeros_like(carry_ref)

    a = _rms(x_ref[...], an_ref[...]).astype(BF16)

    xin = jnp.dot(a, win_ref[:, 0:conv_ch], preferred_element_type=F32)
    c_g = jnp.dot(a, win_ref[:, 2 * conv_ch:3 * conv_ch], preferred_element_type=F32)
    u = c_g * xin
    prev = carry_ref[...]
    row = lax.broadcasted_iota(jnp.int32, u.shape, 0)
    u1 = jnp.where(row == 0, prev[7:8], pltpu.roll(u, 1, 0))
    u2 = jnp.where(row == 0, prev[6:7], jnp.where(row == 1, prev[7:8], pltpu.roll(u, 2, 0)))
    carry_ref[...] = u[tm - 8:tm]
    y = cw_ref[2:3] * u + cw_ref[1:2] * u1 + cw_ref[0:1] * u2
    b_g = jnp.dot(a, win_ref[:, conv_ch:2 * conv_ch], preferred_element_type=F32)
    conv_o = b_g * y
    for g in range(conv_ch // CONV_GROUP_DIM):
        sl = slice(g * CONV_GROUP_DIM, (g + 1) * CONV_GROUP_DIM)
        conv_ref[:, sl] = _rms(conv_o[:, sl], con_ref[:, sl]).astype(BF16)

    m0 = 3 * conv_ch
    lat = jnp.dot(a, win_ref[:, m0:], preferred_element_type=F32)
    ang = pos_ref[...].astype(F32) * freq_ref[...]
    cos_t = jnp.cos(ang) * jnp.abs(sign_ref[...])
    sin_t = jnp.sin(ang) * sign_ref[...]

    def rope(blk):
        return blk * cos_t + pltpu.roll(blk, QK_ROPE, 1) * sin_t

    cq = _rms(lat[:, 0:q_lora], qn_ref[...]).astype(BF16)
    qf = jnp.dot(cq, wuq_ref[...], preferred_element_type=F32)
    ckv = _rms(lat[:, q_lora:q_lora + kv_lora], kvn_ref[...]).astype(BF16)
    kvf = jnp.dot(ckv, wukv_ref[...], preferred_element_type=F32)
    kr_t = rope(lat[:, q_lora + kv_lora:]).T.astype(BF16)
    for h in range(N_HEADS):
        o = h * HEAD_PAD
        q_ref[:, o:o + QK_NOPE] = qf[:, o:o + QK_NOPE].astype(BF16)
        q_ref[:, o + QK_NOPE:o + HEAD_PAD] = rope(qf[:, o + QK_NOPE:o + HEAD_PAD]).astype(BF16)
        kt_ref[o:o + QK_NOPE, :] = kvf[:, h * QK_NOPE:(h + 1) * QK_NOPE].T.astype(BF16)
        kt_ref[o + QK_NOPE:o + HEAD_PAD, :] = kr_t
    v_ref[...] = kvf[:, N_HEADS * QK_NOPE:].astype(BF16)


def _in_proj(x2, pos2, attn_norm, w_in_ext, conv_w, q_norm, w_uq_ext, kv_norm, w_ukv_p,
             conv_out_norm, freq_row, sign_row, *, seq, tm):
    n, d = x2.shape
    conv_ch = conv_w.shape[-1]
    q_lora, kv_lora = q_norm.shape[-1], kv_norm.shape[-1]
    assert seq % tm == 0 and n % seq == 0
    tok = lambda w: pl.BlockSpec((tm, w), lambda i: (i, 0))
    kern = functools.partial(_in_proj_kernel, tiles_per_seq=seq // tm, conv_ch=conv_ch,
                             q_lora=q_lora, kv_lora=kv_lora)
    return pl.pallas_call(
        kern,
        grid=(n // tm,),
        in_specs=[tok(d), tok(1), _resident((1, d)), _resident(w_in_ext.shape),
                  _resident(conv_w.shape), _resident((1, q_lora)), _resident(w_uq_ext.shape),
                  _resident((1, kv_lora)), _resident(w_ukv_p.shape), _resident((1, conv_ch)),
                  _resident((1, LANES)), _resident((1, LANES))],
        out_specs=[tok(conv_ch), tok(N_HEADS * HEAD_PAD),
                   pl.BlockSpec((N_HEADS * HEAD_PAD, tm), lambda i: (0, i)),
                   tok(N_HEADS * V_HEAD)],
        out_shape=[jax.ShapeDtypeStruct((n, conv_ch), BF16),
                   jax.ShapeDtypeStruct((n, N_HEADS * HEAD_PAD), BF16),
                   jax.ShapeDtypeStruct((N_HEADS * HEAD_PAD, n), BF16),
                   jax.ShapeDtypeStruct((n, N_HEADS * V_HEAD), BF16)],
        scratch_shapes=[pltpu.VMEM((8, conv_ch), F32)],
        compiler_params=_cparams(("arbitrary",)),
        name="in_proj",
    )(x2, pos2, attn_norm, w_in_ext, conv_w, q_norm, w_uq_ext, kv_norm, w_ukv_p, conv_out_norm,
      freq_row, sign_row)


def _attn_kernel(q_ref, kt_ref, v_ref, g_ref, o_ref, m_ref, l_ref, acc_ref, *, tk, rc, coef):
    qi = pl.program_id(2)
    tq = q_ref.shape[0]
    n_diag = tq // tk
    m_ref[...] = jnp.full_like(m_ref, NEG_BIG)
    l_ref[...] = jnp.zeros_like(l_ref)
    acc_ref[...] = jnp.zeros_like(acc_ref)

    def chunk_step(kb, r0, mask_offset):
        rows = slice(r0, r0 + rc)
        start = pl.multiple_of(kb * tk, tk)
        kt = kt_ref[:, pl.ds(start, tk)]
        v = v_ref[pl.ds(start, tk), :]
        s = jnp.dot(q_ref[rows, :], kt, preferred_element_type=F32)
        slabs = [s[:, j * LANES:(j + 1) * LANES] for j in range(tk // LANES)]
        if mask_offset is not None:
            r = lax.broadcasted_iota(jnp.int32, (rc, LANES), 0) + r0
            c = lax.broadcasted_iota(jnp.int32, (rc, LANES), 1)
            slabs = [jnp.where(c + (mask_offset + j * LANES) <= r, sl, NEG_BIG)
                     for j, sl in enumerate(slabs)]
        m_old = m_ref[rows, :]
        m_new = jnp.maximum(m_old, jnp.max(functools.reduce(jnp.maximum, slabs), axis=-1,
                                           keepdims=True))
        alpha = jnp.exp2((m_old - m_new) * coef)
        ps = [jnp.exp2((sl - m_new) * coef) for sl in slabs]
        l_ref[rows, :] = alpha * l_ref[rows, :] + jnp.sum(functools.reduce(jnp.add, ps), axis=-1,
                                                          keepdims=True)
        p = jnp.concatenate([x.astype(BF16) for x in ps], axis=1)
        acc_ref[rows, :] = alpha * acc_ref[rows, :] + jnp.dot(p, v, preferred_element_type=F32)
        m_ref[rows, :] = m_new

    n_full = qi * n_diag

    def body(kb, carry):
        for r0 in range(0, tq, rc):
            chunk_step(kb, r0, None)
        return carry

    lax.fori_loop(0, n_full, body, 0)
    for j in range(n_diag):
        for r0 in range(0, tq, rc):
            if r0 + rc - 1 < j * tk:
                continue
            needs_mask = r0 < j * tk + tk - 1
            chunk_step(n_full + j, r0, j * tk if needs_mask else None)

    o = acc_ref[...] / l_ref[...]
    o_ref[...] = _rms(o, g_ref[...]).astype(BF16)


def _attention(q_pad, kt_pad, v, attn_out_norm, *, batch, seq, tq, tk, rc):
    n = q_pad.shape[0]
    assert seq % tq == 0 and tq % tk == 0 and tq % rc == 0 and tk % LANES == 0
    nq = seq // tq
    coef = float(QK_NOPE + QK_ROPE) ** -0.5 * 1.4426950408889634
    kern = functools.partial(_attn_kernel, tk=tk, rc=rc, coef=coef)
    return pl.pallas_call(
        kern,
        grid=(batch, N_HEADS, nq),
        in_specs=[pl.BlockSpec((tq, HEAD_PAD), lambda b, h, i: (b * nq + i, h)),
                  pl.BlockSpec((HEAD_PAD, seq), lambda b, h, i: (h, b)),
                  pl.BlockSpec((seq, V_HEAD), lambda b, h, i: (b, h)),
                  pl.BlockSpec((1, V_HEAD), lambda b, h, i: (0, h))],
        out_specs=pl.BlockSpec((tq, V_HEAD), lambda b, h, i: (b * nq + i, h)),
        out_shape=jax.ShapeDtypeStruct((n, N_HEADS * V_HEAD), BF16),
        scratch_shapes=[pltpu.VMEM((tq, LANES), F32), pltpu.VMEM((tq, LANES), F32),
                        pltpu.VMEM((tq, V_HEAD), F32)],
        compiler_params=_cparams(("arbitrary", "arbitrary", "arbitrary")),
        name="attention",
    )(q_pad, kt_pad, v, attn_out_norm)


def _out_proj_kernel(conv_ref, attn_ref, x_ref, wo_ref, fn_ref, wpqt_ref, keys_ref,
                     h1_ref, ft_ref, sc_ref):
    conv_ch = conv_ref.shape[1]
    mix = jnp.dot(conv_ref[...], wo_ref[0:conv_ch, :], preferred_element_type=F32)
    mix += jnp.dot(attn_ref[...], wo_ref[conv_ch:, :], preferred_element_type=F32)
    h1 = x_ref[...] + mix
    h1_ref[...] = h1
    ft = _rms(h1, fn_ref[...]).T.astype(BF16)
    ft_ref[...] = ft
    qpt = jnp.dot(wpqt_ref[...], ft, preferred_element_type=F32).astype(BF16)
    for hp in range(keys_ref.shape[0]):
        sc = jnp.dot(keys_ref[hp], qpt[hp * N_KEYS:(hp + 1) * N_KEYS, :],
                     preferred_element_type=F32)
        for c in range(sc_ref.shape[1]):
            sc_ref[hp, c] = sc[:, c * LANES:(c + 1) * LANES]


def _out_proj(conv_n, attn_n, x2, w_out, ffn_norm, w_pq_t, keys, *, tm):
    n, d = x2.shape
    nhp = keys.shape[0]
    tok = lambda w: pl.BlockSpec((tm, w), lambda i: (i, 0))
    return pl.pallas_call(
        _out_proj_kernel,
        grid=(n // tm,),
        in_specs=[tok(conv_n.shape[1]), tok(attn_n.shape[1]), tok(d), _resident(w_out.shape),
                  _resident((1, d)), _resident(w_pq_t.shape), _resident(keys.shape)],
        out_specs=[tok(d), pl.BlockSpec((d, tm), lambda i: (0, i)),
                   pl.BlockSpec((nhp, tm // LANES, N_KEYS, LANES), lambda i: (0, i, 0, 0))],
        out_shape=[jax.ShapeDtypeStruct((n, d), F32), jax.ShapeDtypeStruct((d, n), BF16),
                   jax.ShapeDtypeStruct((nhp, n // LANES, N_KEYS, LANES), F32)],
        compiler_params=_cparams(("arbitrary",)),
        name="out_proj",
    )(conv_n, attn_n, x2, w_out, ffn_norm, w_pq_t, keys)


def _sort_network(n):
    pairs = []

    def merge(lo, m, r):
        step = 2 * r
        if step < m:
            merge(lo, m, step)
            merge(lo + r, m, step)
            pairs.extend((i, i + r) for i in range(lo + r, lo + m - r, step))
        else:
            pairs.append((lo, lo + r))

    def sort(lo, m):
        if m > 1:
            sort(lo, m // 2)
            sort(lo + m // 2, m // 2)
            merge(lo, m, 1)

    sort(0, n)
    return pairs


def _top16(s):
    k = PEER_TOPK
    c = s.shape[1]
    g = [s[8 * j:8 * j + 8, :] for j in range(s.shape[0] // 8)]
    assert len(g) == k

    def exchange(i, j):
        g[i], g[j] = jnp.maximum(g[i], g[j]), jnp.minimum(g[i], g[j])

    for i, j in _sort_network(k):
        exchange(i, j)
    for shift in (4, 2, 1):
        other = [pltpu.roll(x, shift, 0) for x in g]
        g = [jnp.maximum(g[j], other[k - 1 - j]) for j in range(k)]
        dist = k // 2
        while dist:
            for i in range(k):
                if not i & dist:
                    exchange(i, i + dist)
            dist //= 2
    rows = lax.broadcasted_iota(jnp.int32, (k, c), 0)
    vals = jnp.zeros((k, c), F32)
    for j in range(k):
        vals = jnp.where(rows == j, jnp.concatenate([g[j], g[j]], axis=0), vals)
    return vals


def _peer_gate_kernel(sc_ref, e1_ref, t2_ref, e2_ref):
    n_chunks = sc_ref.shape[1]

    def one(h, c):
        s1 = sc_ref[2 * h, c]
        s2 = sc_ref[2 * h + 1, c]
        v1 = _top16(s1)
        v2 = _top16(s2)
        cand = jnp.concatenate([v1 + v2[0:1]] + [v1[0:8] + v2[b:b + 1] for b in range(1, 8)]
                               + [v1[0:1] + v2[8:16]], axis=0)
        best = v1[0:1] + v2[0:1]
        pad = jnp.full((N_KEYS - cand.shape[0], cand.shape[1]), -jnp.inf, F32)
        top = _top16(jnp.concatenate([cand, pad], axis=0))
        thr = top[PEER_TOPK - 1:PEER_TOPK]
        zsum = jnp.sum(jnp.exp(top - best), axis=0, keepdims=True)
        tsel = jnp.full_like(v1, jnp.inf)
        for b in range(PEER_TOPK):
            tsel = jnp.where(v1 + v2[b:b + 1] >= thr, v2[b:b + 1], tsel)
        t2 = jnp.full_like(s1, jnp.inf)
        for a in range(PEER_TOPK):
            t2 = jnp.where(s1 == v1[a:a + 1], tsel[a:a + 1], t2)
        e1_ref[h, c] = jnp.exp(s1 - v1[0:1])
        t2_ref[h, c] = t2
        e2_ref[h, c] = jnp.exp(s2 - v2[0:1]) * (0.5 / zsum)

    def body(t, carry):
        one(t // n_chunks, t % n_chunks)
        return carry

    lax.fori_loop(0, PEER_HEADS * n_chunks, body, 0)


def _peer_gate(scores, *, tm):
    nhp, n_chunks, nk, _ = scores.shape
    cpt = tm // LANES
    out = [jax.ShapeDtypeStruct((nhp // 2, n_chunks, nk, LANES), F32)] * 3
    spec = pl.BlockSpec((nhp // 2, cpt, nk, LANES), lambda i: (0, i, 0, 0))
    return pl.pallas_call(
        _peer_gate_kernel,
        grid=(n_chunks // cpt,),
        in_specs=[pl.BlockSpec((nhp, cpt, nk, LANES), lambda i: (0, i, 0, 0))],
        out_specs=[spec, spec, spec],
        out_shape=out,
        compiler_params=_cparams(("arbitrary",)),
        name="peer_gate",
    )(scores)


def _peer_dense_kernel(ft_ref, u_ref, vt_ref, e1_ref, t2_ref, e2_ref, s2_ref, o_ref,
                       h0_ref, h1_ref, w0_ref, w1_ref, *, n_blocks):
    s = pl.program_id(1)
    n_chunks, eb, _ = w0_ref.shape
    per_block = eb // N_KEYS

    @pl.when(s == 0)
    def _():
        o_ref[...] = jnp.zeros_like(o_ref)

    def stage(parity, hidden=True, gate=True, out=True):
        h_new, h_old = (h0_ref, h1_ref) if parity == 0 else (h1_ref, h0_ref)
        w_new, w_old = (w1_ref, w0_ref) if parity == 0 else (w0_ref, w1_ref)
        second = 1 - parity
        pair = (s - 1) // 2
        hm = eb // HIDDEN_ROW_PIECES

        def hidden_piece(m0, c0, k0, nk):
            res = jnp.dot(u_ref[m0:m0 + hm, k0:k0 + nk],
                          ft_ref[k0:k0 + nk, c0 * LANES:(c0 + 2) * LANES],
                          preferred_element_type=F32)
            if k0 == 0:
                h_new[c0, m0:m0 + hm, :] = res[:, :LANES]
                h_new[c0 + 1, m0:m0 + hm, :] = res[:, LANES:]
            else:
                h_new[c0, m0:m0 + hm, :] += res[:, :LANES]
                h_new[c0 + 1, m0:m0 + hm, :] += res[:, LANES:]

        def out_piece(r0, nr):
            w = jnp.concatenate([w_old[c] for c in range(n_chunks)], axis=1)
            o_ref[r0:r0 + nr, :] += jnp.dot(vt_ref[r0:r0 + nr, :], w, preferred_element_type=F32)

        def gate_tile(a, c, part):
            r = per_block * second + a
            keys = slice(part * GATE_TILE_ROWS, (part + 1) * GATE_TILE_ROWS)
            rows = slice(a * N_KEYS + keys.start, a * N_KEYS + keys.stop)
            gate = None
            for h in range(PEER_HEADS):
                t2 = t2_ref[h, c, pair, r:r + 1, :]
                e1 = e1_ref[h, c, pair, r:r + 1, :]
                term = jnp.where(s2_ref[h, c, keys, :] >= t2, e2_ref[h, c, keys, :] * e1, 0.0)
                gate = term if gate is None else gate + term
            x = h_old[c, rows, :]
            gelu2 = x * (1.0 + lax.erf(x * (2.0 ** -0.5)))
            w_new[c, rows, :] = (gate * gelu2).astype(BF16)

        d = o_ref.shape[0]
        nk = d // HIDDEN_K_PIECES
        nr = d // OUT_ROW_PIECES
        pieces = []
        if hidden:
            pieces += [functools.partial(hidden_piece, m0, c0, k0, nk) for k0 in range(0, d, nk)
                       for m0 in range(0, eb, hm) for c0 in range(0, n_chunks, 2)]
        if out:
            pieces += [functools.partial(out_piece, r0, nr) for r0 in range(0, d, nr)]
        tiles = [(a, c, part) for a in range(per_block) for c in range(n_chunks)
                 for part in range(N_KEYS // GATE_TILE_ROWS)] if gate else []
        done = 0
        for q, piece in enumerate(pieces):
            piece()
            want = (q + 1) * len(tiles) // len(pieces)
            for tile in tiles[done:want]:
                gate_tile(*tile)
            done = want

    pl.when(s == 0)(functools.partial(stage, 0, gate=False, out=False))
    pl.when(s == 1)(functools.partial(stage, 1, out=False))
    for parity in (0, 1):
        pl.when((s % 2 == parity) & (s >= 2) & (s < n_blocks))(functools.partial(stage, parity))
    pl.when(s == n_blocks)(functools.partial(stage, n_blocks % 2, hidden=False))
    pl.when(s == n_blocks + 1)(functools.partial(stage, (n_blocks + 1) % 2, hidden=False,
                                                 gate=False))


def _peer_dense(f_t, u_bf, v_t, e1, t2, e2, scores, *, t, eb):
    d, n = f_t.shape
    ne = u_bf.shape[0]
    assert ne % (2 * eb) == 0 and 2 * eb == 8 * N_KEYS
    assert t % (2 * LANES) == 0
    n_blocks = ne // eb
    cpt = t // LANES
    once = dict(pipeline_mode=pl.Buffered(1))
    gspec = pl.BlockSpec((PEER_HEADS, cpt, N_KEYS, LANES), lambda i, s: (0, i, 0, 0), **once)
    pspec = pl.BlockSpec((PEER_HEADS, cpt, N_KEYS // 8, 8, LANES), lambda i, s: (0, i, 0, 0, 0),
                         **once)
    s2spec = pl.BlockSpec((PEER_HEADS, None, cpt, N_KEYS, LANES), lambda i, s: (0, 1, i, 0, 0),
                          **once)
    e1 = e1.reshape(PEER_HEADS, n // LANES, N_KEYS // 8, 8, LANES)
    t2 = t2.reshape(PEER_HEADS, n // LANES, N_KEYS // 8, 8, LANES)
    scores = scores.reshape(PEER_HEADS, 2, n // LANES, N_KEYS, LANES)
    kern = functools.partial(_peer_dense_kernel, n_blocks=n_blocks)
    return pl.pallas_call(
        kern,
        grid=(n // t, n_blocks + 2),
        in_specs=[pl.BlockSpec((d, t), lambda i, s: (0, i), **once),
                  pl.BlockSpec((eb, d), lambda i, s: (jnp.minimum(s, n_blocks - 1), 0)),
                  pl.BlockSpec((None, d, eb), lambda i, s: (jnp.clip(s - 2, 0, n_blocks - 1), 0, 0)),
                  pspec, pspec, gspec, s2spec],
        out_specs=pl.BlockSpec((d, t), lambda i, s: (0, i)),
        out_shape=jax.ShapeDtypeStruct((d, n), F32),
        scratch_shapes=[pltpu.VMEM((cpt, eb, LANES), F32), pltpu.VMEM((cpt, eb, LANES), F32),
                        pltpu.VMEM((cpt, eb, LANES), BF16), pltpu.VMEM((cpt, eb, LANES), BF16)],
        compiler_params=_cparams(("arbitrary", "arbitrary")),
        name="peer_dense",
    )(f_t, u_bf, v_t, e1, t2, e2, scores)


def _ple_kernel(h1_ref, peer_ref, p_ref, pn_ref, wg_ref, wp_ref, fin_ref, o_ref, *, last_layer):
    h2 = h1_ref[...] + peer_ref[...].T
    gate = jax.nn.sigmoid(jnp.dot(_rms(h2, pn_ref[...]).astype(BF16), wg_ref[...],
                                  preferred_element_type=F32))
    proj = jnp.dot(p_ref[...].astype(BF16), wp_ref[...], preferred_element_type=F32)
    h3 = h2 + proj * gate
    o_ref[...] = _rms(h3, fin_ref[...]) if last_layer else h3


def _ple_final(h1, peer, p2, ple_norm, w_gate, w_proj, final_norm, *, tm, last_layer):
    n, d = h1.shape
    tok = lambda w: pl.BlockSpec((tm, w), lambda i: (i, 0))
    return pl.pallas_call(
        functools.partial(_ple_kernel, last_layer=last_layer),
        grid=(n // tm,),
        in_specs=[tok(d), pl.BlockSpec((d, tm), lambda i: (0, i)), tok(p2.shape[1]),
                  _resident((1, d)), _resident(w_gate.shape),
                  _resident(w_proj.shape), _resident((1, d))],
        out_specs=tok(d),
        out_shape=jax.ShapeDtypeStruct((n, d), F32),
        compiler_params=_cparams(("arbitrary",)),
        name="ple_final",
    )(h1, peer, p2, ple_norm, w_gate, w_proj, final_norm)


def _swap_halves(w):
    half = w.shape[-1] // 2
    return jnp.concatenate([w[..., half:], w[..., :half]], axis=-1)


def _tile(n, pref):
    t = min(n, pref)
    assert n % t == 0
    return t


def _tiles(n, seq):
    return dict(
        in_proj=_tile(seq, 512),
        attn_q=_tile(seq, 2048), attn_rows=_tile(seq, 1024), attn_k=_tile(seq, 1024),
        out_proj=_tile(n, 512), peer_gate=_tile(n, 512), ple=_tile(n, 512),
        peer_tokens=_tile(n, 1024),
        peer_experts=4 * N_KEYS)


def kernel(x, p, positions, attn_norm, w_in, conv_w, q_norm, w_uq, kv_norm, w_ukv, conv_out_norm,
           attn_out_norm, w_out, ffn_norm, w_pq, sub_keys, u_tab, v_tab, ple_norm, w_ple_gate,
           w_ple_proj, final_norm):
    batch, seq, d = x.shape
    n = batch * seq
    depth = w_in.shape[0]
    tiles = _tiles(n, seq)
    row = lambda g: g.reshape(1, -1)

    inv_freq = ROPE_THETA ** (-(jnp.arange(0, QK_ROPE, 2, dtype=F32) / QK_ROPE))
    zeros = jnp.zeros((LANES - QK_ROPE,), F32)
    freq_row = row(jnp.concatenate([inv_freq, inv_freq, zeros]))
    sign_row = row(jnp.concatenate([-jnp.ones_like(inv_freq), jnp.ones_like(inv_freq), zeros]))
    pos2 = positions.reshape(n, 1)

    h = x.reshape(n, d)
    for i in range(depth):
        kr_cols = w_in[i][:, -QK_ROPE:]
        w_in_ext = jnp.concatenate([w_in[i], _swap_halves(kr_cols)], axis=-1).astype(BF16)
        uq = w_uq[i].reshape(-1, N_HEADS, QK_NOPE + QK_ROPE)
        uq_rope = uq[..., QK_NOPE:]
        w_uq_ext = jnp.concatenate([uq, _swap_halves(uq_rope)], axis=-1)
        w_uq_ext = w_uq_ext.reshape(-1, N_HEADS * HEAD_PAD).astype(BF16)
        ukv = w_ukv[i].reshape(-1, N_HEADS, QK_NOPE + V_HEAD)
        w_ukv_p = jnp.concatenate([ukv[..., :QK_NOPE].reshape(-1, N_HEADS * QK_NOPE),
                                   ukv[..., QK_NOPE:].reshape(-1, N_HEADS * V_HEAD)],
                                  axis=-1).astype(BF16)
        keys = sub_keys[i].reshape(-1, N_KEYS, sub_keys.shape[-1]).astype(BF16)

        conv_n, q_pad, k_pad, v = _in_proj(
            h, pos2, row(attn_norm[i]), w_in_ext, conv_w[i], row(q_norm[i]), w_uq_ext,
            row(kv_norm[i]), w_ukv_p, row(conv_out_norm[i]), freq_row, sign_row,
            seq=seq, tm=tiles["in_proj"])
        attn_n = _attention(q_pad, k_pad, v, row(attn_out_norm[i]), batch=batch, seq=seq,
                            tq=tiles["attn_q"], tk=tiles["attn_k"], rc=tiles["attn_rows"])
        h1, f_t, scores = _out_proj(conv_n, attn_n, h, w_out[i].astype(BF16), row(ffn_norm[i]),
                                    w_pq[i].T.astype(BF16), keys, tm=tiles["out_proj"])
        e1, t2, e2 = _peer_gate(scores, tm=tiles["peer_gate"])
        eb = tiles["peer_experts"]
        v_blocks = v_tab[i].reshape(-1, eb, d).transpose(0, 2, 1).astype(BF16)
        peer = _peer_dense(f_t, u_tab[i].astype(BF16), v_blocks, e1, t2, e2, scores,
                           t=tiles["peer_tokens"], eb=eb)
        h = _ple_final(h1, peer, p[i].reshape(n, -1), row(ple_norm[i]),
                       w_ple_gate[i].astype(BF16), w_ple_proj[i].astype(BF16),
                       row(final_norm), tm=tiles["ple"], last_layer=i == depth - 1)
    return h.reshape(batch, seq, d)
```
